```python
import math
import jax, jax.numpy as jnp
from jax import lax
import numpy as np

D_MODEL = 1024
BATCH = 4
SEQ = 4096
DEPTH = 4

CHUNK = 64
Q_BLOCK = 128
MEM_LEN = 256
HEAD_DIM = 64
BR_W = 512
N_BRANCH = 4
A_HEADS = BR_W // (2 * HEAD_DIM)
F_HEADS = BR_W // HEAD_DIM
CONV_W = BR_W
CONV_K = 31
SC_W = BR_W
SC_K = 3
IN_COLS = 3 * BR_W + (3 * BR_W + F_HEADS) + 2 * CONV_W + 3 * SC_W
X_HEADS = 4
X_HEAD_DIM = 128
X_W = X_HEADS * X_HEAD_DIM
D_FF = 2816
FFN_K = 3
EPS = 1e-6

kernel_name = "hybrid_parallel_gated_streaming_encoder"


def _rms_norm(x, g):
    xf = x.astype(jnp.float32)
    y = xf * lax.rsqrt(jnp.mean(xf * xf, axis=-1, keepdims=True) + EPS)
    return (y * g.astype(jnp.float32)).astype(x.dtype)


def _layer_norm(x, g, b):
    xf = x.astype(jnp.float32)
    mu = jnp.mean(xf, axis=-1, keepdims=True)
    var = jnp.mean(jnp.square(xf - mu), axis=-1, keepdims=True)
    y = (xf - mu) * lax.rsqrt(var + EPS)
    return (y * g.astype(jnp.float32) + b.astype(jnp.float32)).astype(x.dtype)


def _dwconv_causal(x, w):
    K, C = w.shape
    return lax.conv_general_dilated(x, w[:, None, :].astype(x.dtype), window_strides=(1,),
                                    padding=[(K - 1, 0)], dimension_numbers=('NWC', 'WIO', 'NWC'),
                                    feature_group_count=C)


def _diff_attention(q, k, v, lam, t_pos):
    B, S, H = q.shape[:3]
    nb = S // Q_BLOCK
    scale = HEAD_DIM ** -0.5
    qb = q.reshape(B, nb, Q_BLOCK, H, 2, HEAD_DIM).transpose(1, 0, 2, 3, 4, 5)
    tq = t_pos.reshape(nb, Q_BLOCK)
    k_chunk = t_pos // CHUNK

    def one(args):
        q_blk, tq_blk = args
        s = jnp.einsum('bqhcd,bkhcd->bhcqk', q_blk, k, preferred_element_type=jnp.float32) * scale
        mask = k_chunk[None, :] <= (tq_blk // CHUNK)[:, None]
        p = jax.nn.softmax(jnp.where(mask, s, -jnp.inf), axis=-1)
        a = p[:, :, 0] - lam * p[:, :, 1]
        return jnp.einsum('bhqk,bkhe->bqhe', a.astype(v.dtype), v)

    o = lax.map(one, (qb, tq))
    return o.transpose(1, 0, 2, 3, 4).reshape(B, S, H, 2 * HEAD_DIM)


def _forgetting_attention(q, k, v, log_f, t_pos):
    B, S, H = q.shape[:3]
    nb = S // Q_BLOCK
    scale = HEAD_DIM ** -0.5
    c = jnp.cumsum(log_f.astype(jnp.float32), axis=1)
    c_k = c.transpose(0, 2, 1)
    qb = q.reshape(B, nb, Q_BLOCK, H, HEAD_DIM).transpose(1, 0, 2, 3, 4)
    cb = c.reshape(B, nb, Q_BLOCK, H).transpose(1, 0, 3, 2)
    tq = t_pos.reshape(nb, Q_BLOCK)

    def one(args):
        q_blk, c_blk, tq_blk = args
        s = jnp.einsum('bqhd,bkhd->bhqk', q_blk, k, preferred_element_type=jnp.float32) * scale
        s = s + c_blk[..., None] - c_k[:, :, None, :]
        mask = t_pos[None, :] <= tq_blk[:, None]
        p = jax.nn.softmax(jnp.where(mask, s, -jnp.inf), axis=-1)
        return jnp.einsum('bhqk,bkhd->bqhd', p.astype(v.dtype), v)

    o = lax.map(one, (qb, cb, tq))
    return o.transpose(1, 0, 2, 3, 4).reshape(B, S, H * HEAD_DIM)


def setup_inputs(seed: int = 0) -> dict:
    key = jax.random.key(seed)
    ks = iter(jax.random.split(key, 48))
    L, D = DEPTH, D_MODEL

    def nrm(shape, scale):
        return jax.random.normal(next(ks), shape, jnp.float32) * scale

    def gain(shape):
        return 1.0 + nrm(shape, 0.05)

    return {
        "x": nrm((BATCH, SEQ, D), 1.0),
        "mem": nrm((BATCH, MEM_LEN, D), 1.0),
        "norm_mix_pre": gain((L, D)),
        "norm_mix_post": gain((L, D)),
        "w_in": nrm((L, D, IN_COLS), D ** -0.5),
        "b_fgt": 2.0 + nrm((L, F_HEADS), 0.1),
        "lam_q1": nrm((L, HEAD_DIM), 0.1),
        "lam_k1": nrm((L, HEAD_DIM), 0.1),
        "lam_q2": nrm((L, HEAD_DIM), 0.1),
        "lam_k2": nrm((L, HEAD_DIM), 0.1),
        "diff_norm": gain((L, 2 * HEAD_DIM)),
        "b_glu": nrm((L, 2 * CONV_W), 0.02),
        "conv_dw": nrm((L, CONV_K, CONV_W), CONV_K ** -0.5),
        "conv_dw_b": nrm((L, CONV_W), 0.02),
        "conv_ln_g": gain((L, CONV_W)),
        "conv_ln_b": nrm((L, CONV_W), 0.02),
        "sc_w": nrm((L, SC_K, SC_W), SC_K ** -0.5),
        "w_branch": nrm((L, N_BRANCH, BR_W, D), BR_W ** -0.5),
        "w_gate": nrm((L, D, N_BRANCH * D), D ** -0.5),
        "b_gate": nrm((L, N_BRANCH * D), 0.02),
        "w_out": nrm((L, D, D), D ** -0.5),
        "norm_x_pre": gain((L, D)),
        "norm_x_post": gain((L, D)),
        "norm_mem": gain((L, D)),
        "w_xq": nrm((L, D, X_W), D ** -0.5),
        "w_xkv": nrm((L, D, 2 * X_W), D ** -0.5),
        "w_xo": nrm((L, X_W, D), X_W ** -0.5),
        "norm_ffn_pre": gain((L, D)),
        "norm_ffn_post": gain((L, D)),
        "w_up": nrm((L, D, 2 * D_FF), D ** -0.5),
        "ffn_dw": nrm((L, FFN_K, 2 * D_FF), FFN_K ** -0.5),
        "ffn_dw_b": nrm((L, 2 * D_FF), 0.02),
        "w_down": nrm((L, D_FF, D), D_FF ** -0.5),
    }


def reference(x, mem, norm_mix_pre, norm_mix_post, w_in, b_fgt, lam_q1, lam_k1, lam_q2, lam_k2,
              diff_norm, b_glu, conv_dw, conv_dw_b, conv_ln_g, conv_ln_b, sc_w, w_branch, w_gate,
              b_gate, w_out, norm_x_pre, norm_x_post, norm_mem, w_xq, w_xkv, w_xo, norm_ffn_pre,
              norm_ffn_post, w_up, ffn_dw, ffn_dw_b, w_down):
    B, S, D = x.shape
    t_pos = jnp.arange(S, dtype=jnp.int32)
    sizes = [BR_W, BR_W, BR_W, BR_W, BR_W, BR_W, F_HEADS, 2 * CONV_W, SC_W, SC_W, SC_W]
    cuts = [int(v) for v in np.cumsum(sizes)[:-1]]

    for l in range(DEPTH):
        h = _rms_norm(x, norm_mix_pre[l])
        (a_q, a_k, a_v, f_q, f_k, f_v, f_g, c_u, s_x, s_b, s_c) = jnp.split(h @ w_in[l], cuts, axis=-1)

        lambda_init = 0.8 - 0.6 * math.exp(-0.3 * l)
        lam = (jnp.exp(jnp.sum(lam_q1[l].astype(jnp.float32) * lam_k1[l].astype(jnp.float32)))
               - jnp.exp(jnp.sum(lam_q2[l].astype(jnp.float32) * lam_k2[l].astype(jnp.float32))) + lambda_init)
        ya = _diff_attention(a_q.reshape(B, S, A_HEADS, 2, HEAD_DIM), a_k.reshape(B, S, A_HEADS, 2, HEAD_DIM),
                             a_v.reshape(B, S, A_HEADS, 2 * HEAD_DIM), lam, t_pos)
        ya = (_rms_norm(ya, diff_norm[l]) * (1.0 - lambda_init)).reshape(B, S, BR_W)

        log_f = jax.nn.log_sigmoid(f_g.astype(jnp.float32) + b_fgt[l].astype(jnp.float32))
        yf = _forgetting_attention(f_q.reshape(B, S, F_HEADS, HEAD_DIM), f_k.reshape(B, S, F_HEADS, HEAD_DIM),
                                   f_v.reshape(B, S, F_HEADS, HEAD_DIM), log_f, t_pos)

        c_a, c_g = jnp.split(c_u + b_glu[l], 2, axis=-1)
        yc = _dwconv_causal(c_a * jax.nn.sigmoid(c_g), conv_dw[l]) + conv_dw_b[l]
        yc = jax.nn.silu(_layer_norm(yc, conv_ln_g[l], conv_ln_b[l]))

        ys = s_b * _dwconv_causal(s_c * s_x, sc_w[l])

        br = jnp.stack([ya, yf, yc, ys], axis=2)
        proj = jnp.einsum('bsnc,ncd->bsnd', br, w_branch[l])
        gates = jax.nn.sigmoid(h @ w_gate[l] + b_gate[l]).reshape(B, S, N_BRANCH, D)
        mixed = jnp.sum(gates * proj, axis=2) @ w_out[l]
        x = x + _rms_norm(mixed, norm_mix_post[l])

        hx = _rms_norm(x, norm_x_pre[l])
        m = _rms_norm(mem, norm_mem[l])
        xq = (hx @ w_xq[l]).reshape(B, S, X_HEADS, X_HEAD_DIM)
        xk, xv = jnp.split(m @ w_xkv[l], 2, axis=-1)
        xk = xk.reshape(B, MEM_LEN, X_HEADS, X_HEAD_DIM)
        xv = xv.reshape(B, MEM_LEN, X_HEADS, X_HEAD_DIM)
        sx = jnp.einsum('bqhd,bkhd->bhqk', xq, xk, preferred_element_type=jnp.float32) * (X_HEAD_DIM ** -0.5)
        px = jax.nn.softmax(sx, axis=-1).astype(xv.dtype)
        ox = jnp.einsum('bhqk,bkhd->bqhd', px, xv).reshape(B, S, X_W) @ w_xo[l]
        x = x + _rms_norm(ox, norm_x_post[l])

        hf = _rms_norm(x, norm_ffn_pre[l])
        u = _dwconv_causal(hf @ w_up[l], ffn_dw[l]) + ffn_dw_b[l]
        u_g, u_v = jnp.split(u, 2, axis=-1)
        yff = (jax.nn.silu(u_g) * u_v) @ w_down[l]
        x = x + _rms_norm(yff, norm_ffn_post[l])

    return x
```

```python
import functools
import math

import jax
import jax.numpy as jnp
from jax import lax
from jax.experimental import pallas as pl
from jax.experimental.pallas import tpu as pltpu

F32 = jnp.float32
BF16 = jnp.bfloat16

CHUNK = 64
HEAD_DIM = 64
BR_W = 512
N_BRANCH = 4
F_HEADS = 8
CONV_K = 31
SC_K = 3
X_HEADS = 4
X_HEAD_DIM = 128
FFN_K = 3
EPS = 1e-6

LANES = 128
SUBLANES = 8
HALO = 32
FFN_HALO = 16
VMEM_LIMIT = 56 * 1024 * 1024


def _cparams(*sem):
    return pltpu.CompilerParams(dimension_semantics=sem, vmem_limit_bytes=VMEM_LIMIT)


def _rms(x, g):
    return x * lax.rsqrt(jnp.mean(x * x, axis=-1, keepdims=True) + EPS) * g


def _in_proj_kernel(x_ref, g_ref, w_ref, wfg_ref, y_ref, fg_ref, h_ref):
    @pl.when(pl.program_id(1) == 0)
    def _():
        h = _rms(x_ref[...], g_ref[...]).astype(BF16)
        h_ref[...] = h
        fg_ref[...] = jnp.dot(h, wfg_ref[...], preferred_element_type=F32)

    y_ref[...] = jnp.dot(h_ref[...], w_ref[...], preferred_element_type=F32).astype(BF16)


def _in_proj(x2d, g, w, wfg, *, tm, tn):
    T, D = x2d.shape
    N = w.shape[1]
    tm, tn = min(tm, T), min(tn, N)
    return pl.pallas_call(
        _in_proj_kernel,
        grid=(T // tm, N // tn),
        in_specs=[pl.BlockSpec((tm, D), lambda i, j: (i, 0)),
                  pl.BlockSpec((1, D), lambda i, j: (0, 0)),
                  pl.BlockSpec((D, tn), lambda i, j: (0, j)),
                  pl.BlockSpec((D, LANES), lambda i, j: (0, 0))],
        out_specs=[pl.BlockSpec((tm, tn), lambda i, j: (i, j)),
                   pl.BlockSpec((tm, LANES), lambda i, j: (i, 0))],
        out_shape=[jax.ShapeDtypeStruct((T, N), BF16), jax.ShapeDtypeStruct((T, LANES), F32)],
        scratch_shapes=[pltpu.VMEM((tm, D), BF16)],
        compiler_params=_cparams("parallel", "arbitrary"),
        name="in_proj",
    )(x2d, g, w, wfg)


def _nt_dot(a, b):
    return lax.dot_general(a, b, (((1,), (1,)), ((), ())), preferred_element_type=F32)


def _diff_attn_kernel(q_ref, k_ref, v_ref, lq1_ref, lk1_ref, lq2_ref, lk2_ref, gn_ref, o_ref,
                      m_ref, l_ref, acc_ref, *, blk, lambda_init):
    qi = pl.program_id(2)
    q = q_ref[...] * (HEAD_DIM ** -0.5)
    lo = lax.broadcasted_iota(jnp.int32, (blk, LANES), 1) < HEAD_DIM
    m_ref[...] = jnp.full(m_ref.shape, -jnp.inf, F32)
    l_ref[...] = jnp.zeros(l_ref.shape, F32)
    acc_ref[...] = jnp.zeros(acc_ref.shape, F32)

    def block(j, masked):
        start = pl.multiple_of(j * blk, blk)
        k = k_ref[pl.ds(start, blk), :]
        v = v_ref[pl.ds(start, blk), :]
        for c in range(2):
            kc = jnp.where(lo if c == 0 else jnp.logical_not(lo), k, jnp.zeros_like(k))
            s = _nt_dot(q, kc)
            if masked:
                rq = lax.broadcasted_iota(jnp.int32, (blk, blk), 0) // CHUNK
                ck = lax.broadcasted_iota(jnp.int32, (blk, blk), 1) // CHUNK
                s = jnp.where(ck <= rq, s, -jnp.inf)
            m_prev = m_ref[c]
            m_new = jnp.maximum(m_prev, jnp.max(s, axis=-1, keepdims=True))
            alpha = jnp.exp(m_prev - m_new)
            p = jnp.exp(s - m_new)
            l_ref[c] = alpha * l_ref[c] + jnp.sum(p, axis=-1, keepdims=True)
            acc_ref[c] = alpha * acc_ref[c] + jnp.dot(p.astype(BF16), v, preferred_element_type=F32)
            m_ref[c] = m_new

    def body(j, carry):
        block(j, False)
        return carry

    lax.fori_loop(0, qi, body, 0)
    block(qi, True)

    lam = (jnp.exp(jnp.sum(lq1_ref[...] * lk1_ref[...], axis=-1, keepdims=True))
           - jnp.exp(jnp.sum(lq2_ref[...] * lk2_ref[...], axis=-1, keepdims=True)) + lambda_init)
    o = acc_ref[0] / l_ref[0] - lam * (acc_ref[1] / l_ref[1])
    o_ref[...] = (_rms(o, gn_ref[...]) * (1.0 - lambda_init)).astype(BF16)


def _diff_attn(y3, lq1, lk1, lq2, lk2, gn, *, blk, lambda_init):
    B, S, _ = y3.shape
    blk = min(blk, S)
    H = BR_W // LANES
    vec = pl.BlockSpec((1, HEAD_DIM), lambda b, h, i: (0, 0))
    return pl.pallas_call(
        functools.partial(_diff_attn_kernel, blk=blk, lambda_init=lambda_init),
        grid=(B, H, S // blk),
        in_specs=[pl.BlockSpec((None, blk, LANES), lambda b, h, i: (b, i, h)),
                  pl.BlockSpec((None, S, LANES), lambda b, h, i: (b, 0, H + h)),
                  pl.BlockSpec((None, S, LANES), lambda b, h, i: (b, 0, 2 * H + h)),
                  vec, vec, vec, vec,
                  pl.BlockSpec((1, LANES), lambda b, h, i: (0, 0))],
        out_specs=pl.BlockSpec((None, blk, LANES), lambda b, h, i: (b, i, h)),
        out_shape=jax.ShapeDtypeStruct((B, S, BR_W), BF16),
        scratch_shapes=[pltpu.VMEM((2, blk, 1), F32), pltpu.VMEM((2, blk, 1), F32),
                        pltpu.VMEM((2, blk, LANES), F32)],
        compiler_params=_cparams("parallel", "parallel", "arbitrary"),
        name="diff_attn",
    )(y3, y3, y3, lq1, lk1, lq2, lk2, gn)


def _log_forget_cumsum_kernel(fg_ref, b_ref, c_ref, *, tiles):
    x = fg_ref[...] + b_ref[...]
    c = jnp.minimum(x, 0.0) - jnp.log(1.0 + jnp.exp(-jnp.abs(x)))
    rows = c.shape[0]
    lane = lax.broadcasted_iota(jnp.int32, (rows, LANES), 1)
    d = 1
    while d < LANES:
        c = c + jnp.where(lane >= d, pltpu.roll(c, d, 1), 0.0)
        d *= 2
    tile = lax.broadcasted_iota(jnp.int32, (rows, LANES), 0) % tiles
    tot = jnp.broadcast_to(c[:, LANES - 1:LANES], (rows, LANES))
    off = jnp.where(tile >= 1, pltpu.roll(tot, 1, 0), 0.0)
    d = 1
    while d < tiles:
        off = off + jnp.where(tile >= d, pltpu.roll(off, d, 0), 0.0)
        d *= 2
    c_ref[...] = c + off


def _log_forget_cumsum(fg_rows, b_rows, *, tiles):
    B, R, _ = fg_rows.shape
    return pl.pallas_call(
        functools.partial(_log_forget_cumsum_kernel, tiles=tiles),
        grid=(B,),
        in_specs=[pl.BlockSpec((None, R, LANES), lambda b: (b, 0, 0)),
                  pl.BlockSpec((R, 1), lambda b: (0, 0))],
        out_specs=pl.BlockSpec((None, R, LANES), lambda b: (b, 0, 0)),
        out_shape=jax.ShapeDtypeStruct((B, R, LANES), F32),
        compiler_params=_cparams("parallel"),
        name="log_forget_cumsum",
    )(fg_rows, b_rows)


def _forget_attn_kernel(q_ref, k_ref, v_ref, cq0_ref, cq1_ref, ck0_ref, ck1_ref, o_ref,
                        m_ref, l_ref, acc_ref, *, blk):
    qi = pl.program_id(2)
    q = q_ref[...] * (HEAD_DIM ** -0.5)
    lo = lax.broadcasted_iota(jnp.int32, (blk, LANES), 1) < HEAD_DIM
    cq = (cq0_ref[...], cq1_ref[...])
    ck_refs = (ck0_ref, ck1_ref)
    m_ref[...] = jnp.full(m_ref.shape, -jnp.inf, F32)
    l_ref[...] = jnp.zeros(l_ref.shape, F32)
    acc_ref[...] = jnp.zeros(acc_ref.shape, F32)

    def block(j, masked):
        start = pl.multiple_of(j * blk, blk)
        k = k_ref[pl.ds(start, blk), :]
        v = v_ref[pl.ds(start, blk), :]
        for e in range(2):
            ke = jnp.where(lo if e == 0 else jnp.logical_not(lo), k, jnp.zeros_like(k))
            z = _nt_dot(q, ke) - ck_refs[e][:, pl.ds(start, blk)]
            if masked:
                rq = lax.broadcasted_iota(jnp.int32, (blk, blk), 0)
                ck = lax.broadcasted_iota(jnp.int32, (blk, blk), 1)
                z = jnp.where(ck <= rq, z, -jnp.inf)
            m_prev = m_ref[e]
            m_new = jnp.maximum(m_prev, jnp.max(z, axis=-1, keepdims=True) + cq[e])
            alpha = jnp.exp(m_prev - m_new)
            p = jnp.exp(z - (m_new - cq[e]))
            l_ref[e] = alpha * l_ref[e] + jnp.sum(p, axis=-1, keepdims=True)
            acc_ref[e] = alpha * acc_ref[e] + jnp.dot(p.astype(BF16), v, preferred_element_type=F32)
            m_ref[e] = m_new

    def body(j, carry):
        block(j, False)
        return carry

    lax.fori_loop(0, qi, body, 0)
    block(qi, True)
    o = jnp.where(lo, acc_ref[0] / l_ref[0], acc_ref[1] / l_ref[1])
    o_ref[...] = o.astype(BF16)


def _forget_attn(y3, c, *, blk):
    B, S, _ = y3.shape
    blk = min(blk, S)
    HP = BR_W // LANES
    base = 3 * HP
    c_col = c.reshape(B, F_HEADS, S, 1)
    c_row = c.reshape(B, F_HEADS, 1, S)
    return pl.pallas_call(
        functools.partial(_forget_attn_kernel, blk=blk),
        grid=(B, HP, S // blk),
        in_specs=[pl.BlockSpec((None, blk, LANES), lambda b, h, i: (b, i, base + h)),
                  pl.BlockSpec((None, S, LANES), lambda b, h, i: (b, 0, base + HP + h)),
                  pl.BlockSpec((None, S, LANES), lambda b, h, i: (b, 0, base + 2 * HP + h)),
                  pl.BlockSpec((None, None, blk, 1), lambda b, h, i: (b, 2 * h, i, 0)),
                  pl.BlockSpec((None, None, blk, 1), lambda b, h, i: (b, 2 * h + 1, i, 0)),
                  pl.BlockSpec((None, None, 1, S), lambda b, h, i: (b, 2 * h, 0, 0)),
                  pl.BlockSpec((None, None, 1, S), lambda b, h, i: (b, 2 * h + 1, 0, 0))],
        out_specs=pl.BlockSpec((None, blk, LANES), lambda b, h, i: (b, i, h)),
        out_shape=jax.ShapeDtypeStruct((B, S, BR_W), BF16),
        scratch_shapes=[pltpu.VMEM((2, blk, 1), F32), pltpu.VMEM((2, blk, 1), F32),
                        pltpu.VMEM((2, blk, LANES), F32)],
        compiler_params=_cparams("parallel", "parallel", "arbitrary"),
        name="forget_attn",
    )(y3, y3, y3, c_col, c_col, c_row, c_row)


def _conv_branches_kernel(cu_ref, cuh_ref, sx_ref, sxh_ref, sb_ref, sc_ref, sch_ref,
                          bglu_ref, cw_ref, cb_ref, lng_ref, lnb_ref, scw_ref,
                          yc_ref, ys_ref, g_ref, gs_ref, p_ref, *, tc, rows):
    first = pl.program_id(1) == 0
    n = tc + HALO

    def glu(u):
        u = u.astype(F32) + bglu_ref[...]
        return u[:, :BR_W] * jax.nn.sigmoid(u[:, BR_W:])

    g_ref[pl.ds(0, HALO), :] = jnp.where(first, 0.0, glu(cuh_ref[...]))
    g_ref[pl.ds(HALO, tc), :] = glu(cu_ref[...])
    g_ref[pl.ds(n, SUBLANES), :] = jnp.zeros((SUBLANES, BR_W), F32)
    for b in range(SUBLANES):
        gs_ref[b] = g_ref[pl.ds(b, n), :]

    def body(r, carry):
        r0 = pl.multiple_of(r * rows, rows)
        acc = jnp.zeros((rows, BR_W), F32)
        for k in range(CONV_K):
            a, b = divmod(HALO - (CONV_K - 1) + k, SUBLANES)
            start = pl.multiple_of(r0 + a * SUBLANES, SUBLANES)
            acc = acc + cw_ref[k:k + 1, :] * gs_ref[b, pl.ds(start, rows), :]
        acc = acc + cb_ref[...]
        mu = jnp.mean(acc, axis=-1, keepdims=True)
        var = jnp.mean(jnp.square(acc - mu), axis=-1, keepdims=True)
        yn = (acc - mu) * lax.rsqrt(var + EPS) * lng_ref[...] + lnb_ref[...]
        yc_ref[pl.ds(r0, rows), :] = (yn * jax.nn.sigmoid(yn)).astype(BF16)
        return carry

    lax.fori_loop(0, tc // rows, body, 0)

    ph = sch_ref[...].astype(F32) * sxh_ref[...].astype(F32)
    p_ref[pl.ds(0, HALO), :] = jnp.where(first, 0.0, ph)
    p_ref[pl.ds(HALO, tc), :] = sc_ref[...].astype(F32) * sx_ref[...].astype(F32)
    sacc = scw_ref[0:1, :] * p_ref[pl.ds(HALO - (SC_K - 1), tc), :]
    for k in range(1, SC_K):
        sacc = sacc + scw_ref[k:k + 1, :] * p_ref[pl.ds(HALO - (SC_K - 1) + k, tc), :]
    ys_ref[...] = (sb_ref[...].astype(F32) * sacc).astype(BF16)


def _conv_branches(y3, bglu, cw, cb, lng, lnb, scw, *, tc, rows=32):
    B, S, _ = y3.shape
    tc = min(tc, S)
    cu_blk = 3072 // (2 * BR_W)
    sx_blk = 4096 // BR_W
    hb = tc // HALO

    def cur(cb_):
        return lambda b, i: (b, i, cb_)

    def halo(cb_):
        return lambda b, i: (b, jnp.maximum(i * hb - 1, 0), cb_)

    def full(a):
        return pl.BlockSpec(a.shape, lambda b, i: (0,) * a.ndim)

    out_spec = pl.BlockSpec((None, tc, BR_W), lambda b, i: (b, i, 0))
    return pl.pallas_call(
        functools.partial(_conv_branches_kernel, tc=tc, rows=rows),
        grid=(B, S // tc),
        in_specs=[pl.BlockSpec((None, tc, 2 * BR_W), cur(cu_blk)),
                  pl.BlockSpec((None, HALO, 2 * BR_W), halo(cu_blk)),
                  pl.BlockSpec((None, tc, BR_W), cur(sx_blk)),
                  pl.BlockSpec((None, HALO, BR_W), halo(sx_blk)),
                  pl.BlockSpec((None, tc, BR_W), cur(sx_blk + 1)),
                  pl.BlockSpec((None, tc, BR_W), cur(sx_blk + 2)),
                  pl.BlockSpec((None, HALO, BR_W), halo(sx_blk + 2)),
                  full(bglu), full(cw), full(cb), full(lng), full(lnb), full(scw)],
        out_specs=[out_spec, out_spec],
        out_shape=[jax.ShapeDtypeStruct((B, S, BR_W), BF16)] * 2,
        scratch_shapes=[pltpu.VMEM((tc + HALO + SUBLANES, BR_W), F32),
                        pltpu.VMEM((SUBLANES, tc + HALO, BR_W), F32),
                        pltpu.VMEM((tc + HALO, BR_W), F32)],
        compiler_params=_cparams("parallel", "arbitrary"),
        name="conv_branches",
    )(y3, y3, y3, y3, y3, y3, y3, bglu, cw, cb, lng, lnb, scw)


def _merge_kernel(x_ref, ya_ref, yf_ref, yc_ref, ys_ref, gpre_ref, wb_ref, wg_ref, bg_ref, wo_ref,
                  gpost_ref, o_ref):
    x = x_ref[...]
    D = x.shape[-1]
    h = _rms(x, gpre_ref[...]).astype(BF16)
    mixed = None
    for n, br_ref in enumerate((ya_ref, yf_ref, yc_ref, ys_ref)):
        proj = jnp.dot(br_ref[...], wb_ref[n], preferred_element_type=F32)
        gate = jax.nn.sigmoid(jnp.dot(h, wg_ref[:, n * D:(n + 1) * D], preferred_element_type=F32)
                              + bg_ref[:, n * D:(n + 1) * D])
        mixed = gate * proj if mixed is None else mixed + gate * proj
    out = jnp.dot(mixed.astype(BF16), wo_ref[...], preferred_element_type=F32)
    o_ref[...] = x + _rms(out, gpost_ref[...])


def _const_spec(a):
    return pl.BlockSpec(a.shape, lambda i: (0,) * a.ndim, pipeline_mode=pl.Buffered(1))


def _merge(x2d, ya, yf, yc, ys, gpre, wb, wg, bg, wo, gpost, *, tm):
    T, D = x2d.shape
    tm = min(tm, T)
    row = lambda w: pl.BlockSpec((tm, w), lambda i: (i, 0))
    return pl.pallas_call(
        _merge_kernel,
        grid=(T // tm,),
        in_specs=[row(D), row(BR_W), row(BR_W), row(BR_W), row(BR_W),
                  _const_spec(gpre), _const_spec(wb), _const_spec(wg), _const_spec(bg),
                  _const_spec(wo), _const_spec(gpost)],
        out_specs=row(D),
        out_shape=jax.ShapeDtypeStruct((T, D), F32),
        compiler_params=_cparams("parallel"),
        name="merge",
    )(x2d, ya, yf, yc, ys, gpre, wb, wg, bg, wo, gpost)


def _mem_kv_kernel(m_ref, g_ref, w_ref, o_ref):
    m = _rms(m_ref[...], g_ref[...]).astype(BF16)
    o_ref[...] = jnp.dot(m, w_ref[...], preferred_element_type=F32).astype(BF16)


def _mem_kv(mem2d, g, w, *, tm):
    T, D = mem2d.shape
    tm = min(tm, T)
    return pl.pallas_call(
        _mem_kv_kernel,
        grid=(T // tm,),
        in_specs=[pl.BlockSpec((tm, D), lambda i: (i, 0)), _const_spec(g), _const_spec(w)],
        out_specs=pl.BlockSpec((tm, w.shape[1]), lambda i: (i, 0)),
        out_shape=jax.ShapeDtypeStruct((T, w.shape[1]), BF16),
        compiler_params=_cparams("parallel"),
        name="mem_kv",
    )(mem2d, g, w)


def _cross_attn_kernel(x_ref, kv_ref, gpre_ref, wq_ref, wo_ref, gpost_ref, o_ref, oh_ref):
    x = x_ref[...]
    hx = _rms(x, gpre_ref[...]).astype(BF16)
    q = (jnp.dot(hx, wq_ref[...], preferred_element_type=F32) * (X_HEAD_DIM ** -0.5)).astype(BF16)
    xw = X_HEADS * X_HEAD_DIM
    for h in range(X_HEADS):
        sl = slice(h * X_HEAD_DIM, (h + 1) * X_HEAD_DIM)
        s = _nt_dot(q[:, sl], kv_ref[:, sl])
        p = jnp.exp(s - jnp.max(s, axis=-1, keepdims=True))
        o = jnp.dot(p.astype(BF16), kv_ref[:, xw + h * X_HEAD_DIM:xw + (h + 1) * X_HEAD_DIM],
                    preferred_element_type=F32)
        oh_ref[:, sl] = (o / jnp.sum(p, axis=-1, keepdims=True)).astype(BF16)
    ox = jnp.dot(oh_ref[...], wo_ref[...], preferred_element_type=F32)
    o_ref[...] = x + _rms(ox, gpost_ref[...])


def _cross_attn(x2d, kv3, gpre, wq, wo, gpost, *, tm, seq):
    T, D = x2d.shape
    tm = min(tm, seq)
    per_b = seq // tm
    M, W = kv3.shape[1:]
    return pl.pallas_call(
        _cross_attn_kernel,
        grid=(T // tm,),
        in_specs=[pl.BlockSpec((tm, D), lambda i: (i, 0)),
                  pl.BlockSpec((None, M, W), lambda i: (i // per_b, 0, 0)),
                  _const_spec(gpre), _const_spec(wq), _const_spec(wo), _const_spec(gpost)],
        out_specs=pl.BlockSpec((tm, D), lambda i: (i, 0)),
        out_shape=jax.ShapeDtypeStruct((T, D), F32),
        scratch_shapes=[pltpu.VMEM((tm, X_HEADS * X_HEAD_DIM), BF16)],
        compiler_params=_cparams("parallel"),
        name="cross_attn",
    )(x2d, kv3, gpre, wq, wo, gpost)


def _ffn_kernel(x_ref, xh_ref, gpre_ref, wug_ref, wuv_ref, dwg_ref, dwv_ref, dbg_ref, dbv_ref,
                wd_ref, gpost_ref, o_ref, hf_ref, acc_ref, *, tm, per_b, nch):
    first = (pl.program_id(0) % per_b) == 0
    x = x_ref[...]
    hf_ref[pl.ds(0, FFN_HALO), :] = _rms(xh_ref[...], gpre_ref[...]).astype(BF16)
    hf_ref[pl.ds(FFN_HALO, tm), :] = _rms(x, gpre_ref[...]).astype(BF16)
    acc_ref[...] = jnp.zeros(acc_ref.shape, F32)
    live = (lax.broadcasted_iota(jnp.int32, (tm + FFN_HALO, 1), 0) >= FFN_HALO) | jnp.logical_not(first)

    def conv(up, dw, db):
        up = jnp.where(live, up, 0.0)
        u = dw[FFN_K - 1:FFN_K, :] * up[FFN_HALO:, :]
        for k in range(FFN_K - 1):
            sh = FFN_K - 1 - k
            u = u + dw[k:k + 1, :] * up[FFN_HALO - sh:FFN_HALO - sh + tm, :]
        return u + db

    def body(c, carry):
        hf = hf_ref[...]
        ug = conv(jnp.dot(hf, wug_ref[c], preferred_element_type=F32), dwg_ref[c], dbg_ref[c])
        uv = conv(jnp.dot(hf, wuv_ref[c], preferred_element_type=F32), dwv_ref[c], dbv_ref[c])
        act = (ug * jax.nn.sigmoid(ug) * uv).astype(BF16)
        acc_ref[...] += jnp.dot(act, wd_ref[c], preferred_element_type=F32)
        return carry

    lax.fori_loop(0, nch, body, 0)
    o_ref[...] = x + _rms(acc_ref[...], gpost_ref[...])


def _ffn(x2d, gpre, wug, wuv, dwg, dwv, dbg, dbv, wd, gpost, *, tm, seq):
    T, D = x2d.shape
    tm = min(tm, seq)
    per_b = seq // tm
    nch = wug.shape[0]
    hb = tm // FFN_HALO
    return pl.pallas_call(
        functools.partial(_ffn_kernel, tm=tm, per_b=per_b, nch=nch),
        grid=(T // tm,),
        in_specs=[pl.BlockSpec((tm, D), lambda i: (i, 0)),
                  pl.BlockSpec((FFN_HALO, D), lambda i: (jnp.maximum(i * hb - 1, 0), 0)),
                  _const_spec(gpre), _const_spec(wug), _const_spec(wuv), _const_spec(dwg),
                  _const_spec(dwv), _const_spec(dbg), _const_spec(dbv), _const_spec(wd),
                  _const_spec(gpost)],
        out_specs=pl.BlockSpec((tm, D), lambda i: (i, 0)),
        out_shape=jax.ShapeDtypeStruct((T, D), F32),
        scratch_shapes=[pltpu.VMEM((tm + FFN_HALO, D), BF16), pltpu.VMEM((tm, D), F32)],
        compiler_params=_cparams("parallel"),
        name="ffn",
    )(x2d, x2d, gpre, wug, wuv, dwg, dwv, dbg, dbv, wd, gpost)


FFN_CHUNK = 256


def kernel(x, mem, norm_mix_pre, norm_mix_post, w_in, b_fgt, lam_q1, lam_k1, lam_q2, lam_k2, diff_norm, b_glu, conv_dw, conv_dw_b, conv_ln_g, conv_ln_b, sc_w, w_branch, w_gate, b_gate, w_out, norm_x_pre, norm_x_post, norm_mem, w_xq, w_xkv, w_xo, norm_ffn_pre, norm_ffn_post, w_up, ffn_dw, ffn_dw_b, w_down):
    B, S, D = x.shape
    depth = w_in.shape[0]
    T = B * S
    d_ff = w_down.shape[1]
    nch = d_ff // FFN_CHUNK
    fg_lo = 6 * BR_W
    fg_hi = fg_lo + F_HEADS
    tiles = S // LANES
    row = lambda a: a.reshape(1, -1)

    x2d = x.reshape(T, D)
    mem2d = mem.reshape(-1, D)
    for l in range(depth):
        lambda_init = 0.8 - 0.6 * math.exp(-0.3 * l)
        w_main = jnp.concatenate([w_in[l][:, :fg_lo], w_in[l][:, fg_hi:]], axis=1).astype(BF16)
        w_fg = jnp.pad(w_in[l][:, fg_lo:fg_hi], ((0, 0), (0, LANES - F_HEADS))).astype(BF16)
        wu = w_up[l].astype(BF16).reshape(D, 2, nch, FFN_CHUNK).transpose(1, 2, 0, 3)
        dw = ffn_dw[l].reshape(FFN_K, 2, nch, FFN_CHUNK).transpose(1, 2, 0, 3)
        db = ffn_dw_b[l].reshape(2, nch, 1, FFN_CHUNK)
        wd = w_down[l].astype(BF16).reshape(nch, FFN_CHUNK, D)

        y, fg = _in_proj(x2d, row(norm_mix_pre[l]), w_main, w_fg, tm=1024, tn=512)
        y3 = y.reshape(B, S, -1)
        ya = _diff_attn(y3, row(lam_q1[l]), row(lam_k1[l]), row(lam_q2[l]), row(lam_k2[l]),
                        row(diff_norm[l]), blk=512, lambda_init=lambda_init)
        fg_rows = fg.reshape(B, S, LANES)[:, :, :F_HEADS].transpose(0, 2, 1).reshape(B, F_HEADS * tiles, LANES)
        b_rows = jnp.repeat(b_fgt[l], tiles).reshape(F_HEADS * tiles, 1)
        c = _log_forget_cumsum(fg_rows, b_rows, tiles=tiles).reshape(B, F_HEADS, S)
        yf = _forget_attn(y3, c, blk=512)
        yc, ys = _conv_branches(y3, row(b_glu[l]), conv_dw[l], row(conv_dw_b[l]), row(conv_ln_g[l]),
                                row(conv_ln_b[l]), sc_w[l], tc=512)
        x2d = _merge(x2d, ya.reshape(T, BR_W), yf.reshape(T, BR_W), yc.reshape(T, BR_W),
                     ys.reshape(T, BR_W), row(norm_mix_pre[l]), w_branch[l].astype(BF16),
                     w_gate[l].astype(BF16), row(b_gate[l]), w_out[l].astype(BF16),
                     row(norm_mix_post[l]), tm=512)

        kv = _mem_kv(mem2d, row(norm_mem[l]), w_xkv[l].astype(BF16), tm=256)
        x2d = _cross_attn(x2d, kv.reshape(B, -1, kv.shape[-1]), row(norm_x_pre[l]), w_xq[l].astype(BF16),
                          w_xo[l].astype(BF16), row(norm_x_post[l]), tm=512, seq=S)

        x2d = _ffn(x2d, row(norm_ffn_pre[l]), wu[0], wu[1], dw[0], dw[1], db[0], db[1], wd,
                   row(norm_ffn_post[l]), tm=512, seq=S)
    return x2d.reshape(B, S, D)
```

```python
import functools
import math

import jax
import jax.numpy as jnp
from jax import lax
from jax.experimental import pallas as pl
from jax.experimental.pallas import tpu as pltpu

F32 = jnp.float32
BF16 = jnp.bfloat16

CHUNK = 64
HEAD_DIM = 64
BR_W = 512
N_BRANCH = 4
F_HEADS = 8
CONV_K = 31
SC_K = 3
X_HEADS = 4
X_HEAD_DIM = 128
FFN_K = 3
EPS = 1e-6

LANES = 128
SUBLANES = 8
HALO = 32
FFN_HALO = 16
ONES_ROWS = 16
LOG2E = 1.4426950408889634
VMEM_LIMIT = 56 * 1024 * 1024


def _cparams(*sem):
    return pltpu.CompilerParams(dimension_semantics=sem, vmem_limit_bytes=VMEM_LIMIT)


def _rms(x, g):
    return x * lax.rsqrt(jnp.mean(x * x, axis=-1, keepdims=True) + EPS) * g


def _in_proj_kernel(x_ref, g_ref, w_ref, cs_ref, wfg_ref, y_ref, fg_ref, h_ref):
    @pl.when(pl.program_id(1) == 0)
    def _():
        h = _rms(x_ref[...], g_ref[...]).astype(BF16)
        h_ref[...] = h
        fg_ref[...] = jnp.dot(h, wfg_ref[...], preferred_element_type=F32)

    y_ref[...] = (jnp.dot(h_ref[...], w_ref[...], preferred_element_type=F32) * cs_ref[...]).astype(BF16)


def _in_proj(x2d, g, w, cs, wfg, *, tm, tn):
    T, D = x2d.shape
    N = w.shape[1]
    tm, tn = min(tm, T), min(tn, N)
    return pl.pallas_call(
        _in_proj_kernel,
        grid=(T // tm, N // tn),
        in_specs=[pl.BlockSpec((tm, D), lambda i, j: (i, 0)),
                  pl.BlockSpec((1, D), lambda i, j: (0, 0)),
                  pl.BlockSpec((D, tn), lambda i, j: (0, j)),
                  pl.BlockSpec((1, tn), lambda i, j: (0, j)),
                  pl.BlockSpec((D, LANES), lambda i, j: (0, 0))],
        out_specs=[pl.BlockSpec((tm, tn), lambda i, j: (i, j)),
                   pl.BlockSpec((tm, LANES), lambda i, j: (i, 0))],
        out_shape=[jax.ShapeDtypeStruct((T, N), BF16), jax.ShapeDtypeStruct((T, LANES), F32)],
        scratch_shapes=[pltpu.VMEM((tm, D), BF16)],
        compiler_params=_cparams("parallel", "arbitrary"),
        name="in_proj",
    )(x2d, g, w, cs, wfg)


def _split_rows(qT):
    row = lax.broadcasted_iota(jnp.int32, qT.shape, 0)
    zero = jnp.zeros_like(qT)
    return jnp.where(row < HEAD_DIM, qT, zero), jnp.where(row >= HEAD_DIM, qT, zero)


def _online_softmax_step(zT, m_ref, acc_ref, idx, vT_aug, shift=None):
    m_prev = m_ref[idx]
    z_max = jnp.max(zT, axis=0, keepdims=True)
    m_new = jnp.maximum(m_prev, z_max if shift is None else z_max + shift)
    alpha = jnp.exp2(m_prev - m_new)
    pT = jnp.exp2(zT - (m_new if shift is None else m_new - shift))
    acc_ref[idx] = alpha * acc_ref[idx] + jnp.dot(vT_aug, pT.astype(BF16), preferred_element_type=F32)
    m_ref[idx] = m_new


def _diff_attn_kernel(qT_ref, k_ref, vT_ref, lq1_ref, lk1_ref, lq2_ref, lk2_ref, gn_ref, o_ref,
                      m_ref, acc_ref, *, blk, nh, lambda_init):
    qi = pl.program_id(2)
    qTc = []
    for h in range(nh):
        qTc += _split_rows(qT_ref[h * LANES:(h + 1) * LANES, :])
    m_ref[...] = jnp.full(m_ref.shape, -jnp.inf, F32)
    acc_ref[...] = jnp.zeros(acc_ref.shape, F32)
    ones = jnp.ones((ONES_ROWS, blk), BF16)

    def block(j, masked):
        start = pl.multiple_of(j * blk, blk)
        ks = [k_ref[pl.ds(start, blk), h * LANES:(h + 1) * LANES] for h in range(nh)]
        vTs = [jnp.concatenate([vT_ref[h * LANES:(h + 1) * LANES, pl.ds(start, blk)], ones], axis=0)
               for h in range(nh)]
        sTs = [jnp.dot(ks[c // 2], qTc[c], preferred_element_type=F32) for c in range(2 * nh)]
        for c in range(2 * nh):
            sT = sTs[c]
            if masked:
                kc = lax.broadcasted_iota(jnp.int32, (blk, blk), 0) // CHUNK
                qc = lax.broadcasted_iota(jnp.int32, (blk, blk), 1) // CHUNK
                sT = jnp.where(kc <= qc, sT, -jnp.inf)
            _online_softmax_step(sT, m_ref, acc_ref, c, vTs[c // 2])

    def body(j, carry):
        block(j, False)
        return carry

    lax.fori_loop(0, qi, body, 0)
    block(qi, True)

    lam = (jnp.exp(jnp.sum(lq1_ref[...] * lk1_ref[...], axis=-1, keepdims=True))
           - jnp.exp(jnp.sum(lq2_ref[...] * lk2_ref[...], axis=-1, keepdims=True)) + lambda_init)
    for h in range(nh):
        a0, a1 = acc_ref[2 * h], acc_ref[2 * h + 1]
        oT = a0[:LANES] / a0[LANES:LANES + 1] - lam * (a1[:LANES] / a1[LANES:LANES + 1])
        yT = oT * lax.rsqrt(jnp.mean(oT * oT, axis=0, keepdims=True) + EPS) * gn_ref[...]
        o_ref[:, h * LANES:(h + 1) * LANES] = (yT * (1.0 - lambda_init)).T.astype(BF16)


def _diff_attn(qT, y3, vT, lq1, lk1, lq2, lk2, gn_col, *, blk, nh, lambda_init):
    B, S, _ = y3.shape
    blk = min(blk, S)
    W = nh * LANES
    k_base = BR_W // W
    vec = pl.BlockSpec((1, HEAD_DIM), lambda b, g, i: (0, 0))
    return pl.pallas_call(
        functools.partial(_diff_attn_kernel, blk=blk, nh=nh, lambda_init=lambda_init),
        grid=(B, BR_W // W, S // blk),
        in_specs=[pl.BlockSpec((None, W, blk), lambda b, g, i: (b, g, i)),
                  pl.BlockSpec((None, S, W), lambda b, g, i: (b, 0, k_base + g)),
                  pl.BlockSpec((None, W, S), lambda b, g, i: (b, g, 0)),
                  vec, vec, vec, vec,
                  pl.BlockSpec((LANES, 1), lambda b, g, i: (0, 0))],
        out_specs=pl.BlockSpec((None, blk, W), lambda b, g, i: (b, i, g)),
        out_shape=jax.ShapeDtypeStruct((B, S, BR_W), BF16),
        scratch_shapes=[pltpu.VMEM((2 * nh, 1, blk), F32),
                        pltpu.VMEM((2 * nh, LANES + ONES_ROWS, blk), F32)],
        compiler_params=_cparams("parallel", "parallel", "arbitrary"),
        name="diff_attn",
    )(qT, y3, vT, lq1, lk1, lq2, lk2, gn_col)


def _log_forget_cumsum_kernel(fg_ref, b_ref, c_ref, *, tiles):
    x = fg_ref[...] + b_ref[...]
    c = jnp.minimum(x, 0.0) - jnp.log(1.0 + jnp.exp(-jnp.abs(x)))
    rows = c.shape[0]
    lane = lax.broadcasted_iota(jnp.int32, (rows, LANES), 1)
    d = 1
    while d < LANES:
        c = c + jnp.where(lane >= d, pltpu.roll(c, d, 1), 0.0)
        d *= 2
    tile = lax.broadcasted_iota(jnp.int32, (rows, LANES), 0) % tiles
    tot = jnp.broadcast_to(c[:, LANES - 1:LANES], (rows, LANES))
    off = jnp.where(tile >= 1, pltpu.roll(tot, 1, 0), 0.0)
    d = 1
    while d < tiles:
        off = off + jnp.where(tile >= d, pltpu.roll(off, d, 0), 0.0)
        d *= 2
    c_ref[...] = (c + off) * LOG2E


def _log_forget_cumsum(fg_rows, b_rows, *, tiles):
    B, R, _ = fg_rows.shape
    return pl.pallas_call(
        functools.partial(_log_forget_cumsum_kernel, tiles=tiles),
        grid=(B,),
        in_specs=[pl.BlockSpec((None, R, LANES), lambda b: (b, 0, 0)),
                  pl.BlockSpec((R, 1), lambda b: (0, 0))],
        out_specs=pl.BlockSpec((None, R, LANES), lambda b: (b, 0, 0)),
        out_shape=jax.ShapeDtypeStruct((B, R, LANES), F32),
        compiler_params=_cparams("parallel"),
        name="log_forget_cumsum",
    )(fg_rows, b_rows)


def _forget_attn_kernel(qT_ref, k_ref, vT_ref, c_ref, o_ref, m_ref, acc_ref, ck_ref, *, blk, npair):
    qi = pl.program_id(2)
    nhead = 2 * npair
    S = c_ref.shape[-1]

    @pl.when(qi == 0)
    def _():
        for h in range(nhead):
            ck_ref[h] = jnp.broadcast_to(c_ref[h:h + 1, :], (LANES, S)).T

    qTh = []
    for p in range(npair):
        qTh += _split_rows(qT_ref[p * LANES:(p + 1) * LANES, :])
    q0 = pl.multiple_of(qi * blk, blk)
    cq = [c_ref[h:h + 1, pl.ds(q0, blk)] for h in range(nhead)]
    m_ref[...] = jnp.full(m_ref.shape, -jnp.inf, F32)
    acc_ref[...] = jnp.zeros(acc_ref.shape, F32)
    ones = jnp.ones((ONES_ROWS, blk), BF16)

    def block(j, masked):
        start = pl.multiple_of(j * blk, blk)
        ks = [k_ref[pl.ds(start, blk), p * LANES:(p + 1) * LANES] for p in range(npair)]
        vTs = [jnp.concatenate([vT_ref[h * HEAD_DIM:(h + 1) * HEAD_DIM, pl.ds(start, blk)], ones], axis=0)
               for h in range(nhead)]
        sTs = [jnp.dot(ks[h // 2], qTh[h], preferred_element_type=F32) for h in range(nhead)]
        for h in range(nhead):
            ck = ck_ref[h, pl.ds(start, blk), :]
            zT = sTs[h] - jnp.concatenate([ck] * (blk // LANES), axis=1)
            if masked:
                kk = lax.broadcasted_iota(jnp.int32, (blk, blk), 0)
                qq = lax.broadcasted_iota(jnp.int32, (blk, blk), 1)
                zT = jnp.where(kk <= qq, zT, -jnp.inf)
            _online_softmax_step(zT, m_ref, acc_ref, h, vTs[h], shift=cq[h])

    def body(j, carry):
        block(j, False)
        return carry

    lax.fori_loop(0, qi, body, 0)
    block(qi, True)

    for p in range(npair):
        a0, a1 = acc_ref[2 * p], acc_ref[2 * p + 1]
        oT = jnp.concatenate([a0[:HEAD_DIM] / a0[HEAD_DIM:HEAD_DIM + 1],
                              a1[:HEAD_DIM] / a1[HEAD_DIM:HEAD_DIM + 1]], axis=0)
        o_ref[:, p * LANES:(p + 1) * LANES] = oT.T.astype(BF16)


def _forget_attn(qT, y3, vT, c, *, blk, npair):
    B, S, _ = y3.shape
    blk = min(blk, S)
    W = npair * LANES
    groups = BR_W // W
    k_base = (4 * BR_W) // W
    c4 = c.reshape(B, groups, 2 * npair, S)
    return pl.pallas_call(
        functools.partial(_forget_attn_kernel, blk=blk, npair=npair),
        grid=(B, groups, S // blk),
        in_specs=[pl.BlockSpec((None, W, blk), lambda b, g, i: (b, g, i)),
                  pl.BlockSpec((None, S, W), lambda b, g, i: (b, 0, k_base + g)),
                  pl.BlockSpec((None, W, S), lambda b, g, i: (b, g, 0)),
                  pl.BlockSpec((None, None, 2 * npair, S), lambda b, g, i: (b, g, 0, 0))],
        out_specs=pl.BlockSpec((None, blk, W), lambda b, g, i: (b, i, g)),
        out_shape=jax.ShapeDtypeStruct((B, S, BR_W), BF16),
        scratch_shapes=[pltpu.VMEM((2 * npair, 1, blk), F32),
                        pltpu.VMEM((2 * npair, HEAD_DIM + ONES_ROWS, blk), F32),
                        pltpu.VMEM((2 * npair, S, LANES), F32)],
        compiler_params=_cparams("parallel", "arbitrary", "arbitrary"),
        name="forget_attn",
    )(qT, y3, vT, c4)


def _conv_branches_kernel(cu_ref, cuh_ref, sx_ref, sxh_ref, sb_ref, sc_ref, sch_ref,
                          bglu_ref, cw_ref, cb_ref, lng_ref, lnb_ref, scw_ref,
                          yc_ref, ys_ref, g_ref, gs_ref, p_ref, *, tc, rows):
    first = pl.program_id(1) == 0
    n = tc + HALO

    def glu(u):
        u = u.astype(F32) + bglu_ref[...]
        return u[:, :BR_W] * jax.nn.sigmoid(u[:, BR_W:])

    g_ref[pl.ds(0, HALO), :] = jnp.where(first, 0.0, glu(cuh_ref[...]))
    g_ref[pl.ds(HALO, tc), :] = glu(cu_ref[...])
    g_ref[pl.ds(n, SUBLANES), :] = jnp.zeros((SUBLANES, BR_W), F32)
    for b in range(SUBLANES):
        gs_ref[b] = g_ref[pl.ds(b, n), :]

    def body(r, carry):
        r0 = pl.multiple_of(r * rows, rows)
        acc = jnp.zeros((rows, BR_W), F32)
        for k in range(CONV_K):
            a, b = divmod(HALO - (CONV_K - 1) + k, SUBLANES)
            start = pl.multiple_of(r0 + a * SUBLANES, SUBLANES)
            acc = acc + cw_ref[k:k + 1, :] * gs_ref[b, pl.ds(start, rows), :]
        acc = acc + cb_ref[...]
        mu = jnp.mean(acc, axis=-1, keepdims=True)
        var = jnp.mean(jnp.square(acc - mu), axis=-1, keepdims=True)
        yn = (acc - mu) * lax.rsqrt(var + EPS) * lng_ref[...] + lnb_ref[...]
        yc_ref[pl.ds(r0, rows), :] = (yn * jax.nn.sigmoid(yn)).astype(BF16)
        return carry

    lax.fori_loop(0, tc // rows, body, 0)

    ph = sch_ref[...].astype(F32) * sxh_ref[...].astype(F32)
    p_ref[pl.ds(0, HALO), :] = jnp.where(first, 0.0, ph)
    p_ref[pl.ds(HALO, tc), :] = sc_ref[...].astype(F32) * sx_ref[...].astype(F32)
    sacc = scw_ref[0:1, :] * p_ref[pl.ds(HALO - (SC_K - 1), tc), :]
    for k in range(1, SC_K):
        sacc = sacc + scw_ref[k:k + 1, :] * p_ref[pl.ds(HALO - (SC_K - 1) + k, tc), :]
    ys_ref[...] = (sb_ref[...].astype(F32) * sacc).astype(BF16)


def _conv_branches(y3, bglu, cw, cb, lng, lnb, scw, *, tc, rows=32):
    B, S, _ = y3.shape
    tc = min(tc, S)
    cu_blk = 3072 // (2 * BR_W)
    sx_blk = 4096 // BR_W
    hb = tc // HALO

    def cur(cb_):
        return lambda b, i: (b, i, cb_)

    def halo(cb_):
        return lambda b, i: (b, jnp.maximum(i * hb - 1, 0), cb_)

    def full(a):
        return pl.BlockSpec(a.shape, lambda b, i: (0,) * a.ndim)

    out_spec = pl.BlockSpec((None, tc, BR_W), lambda b, i: (b, i, 0))
    return pl.pallas_call(
        functools.partial(_conv_branches_kernel, tc=tc, rows=rows),
        grid=(B, S // tc),
        in_specs=[pl.BlockSpec((None, tc, 2 * BR_W), cur(cu_blk)),
                  pl.BlockSpec((None, HALO, 2 * BR_W), halo(cu_blk)),
                  pl.BlockSpec((None, tc, BR_W), cur(sx_blk)),
                  pl.BlockSpec((None, HALO, BR_W), halo(sx_blk)),
                  pl.BlockSpec((None, tc, BR_W), cur(sx_blk + 1)),
                  pl.BlockSpec((None, tc, BR_W), cur(sx_blk + 2)),
                  pl.BlockSpec((None, HALO, BR_W), halo(sx_blk + 2)),
                  full(bglu), full(cw), full(cb), full(lng), full(lnb), full(scw)],
        out_specs=[out_spec, out_spec],
        out_shape=[jax.ShapeDtypeStruct((B, S, BR_W), BF16)] * 2,
        scratch_shapes=[pltpu.VMEM((tc + HALO + SUBLANES, BR_W), F32),
                        pltpu.VMEM((SUBLANES, tc + HALO, BR_W), F32),
                        pltpu.VMEM((tc + HALO, BR_W), F32)],
        compiler_params=_cparams("parallel", "arbitrary"),
        name="conv_branches",
    )(y3, y3, y3, y3, y3, y3, y3, bglu, cw, cb, lng, lnb, scw)


def _merge_kernel(x_ref, ya_ref, yf_ref, yc_ref, ys_ref, gpre_ref, wb_ref, wg_ref, bg_ref, wo_ref,
                  gpost_ref, o_ref):
    x = x_ref[...]
    D = x.shape[-1]
    h = _rms(x, gpre_ref[...]).astype(BF16)
    mixed = None
    for n, br_ref in enumerate((ya_ref, yf_ref, yc_ref, ys_ref)):
        proj = jnp.dot(br_ref[...], wb_ref[n], preferred_element_type=F32)
        gate = jax.nn.sigmoid(jnp.dot(h, wg_ref[:, n * D:(n + 1) * D], preferred_element_type=F32)
                              + bg_ref[:, n * D:(n + 1) * D])
        mixed = gate * proj if mixed is None else mixed + gate * proj
    out = jnp.dot(mixed.astype(BF16), wo_ref[...], preferred_element_type=F32)
    o_ref[...] = x + _rms(out, gpost_ref[...])


def _const_spec(a):
    return pl.BlockSpec(a.shape, lambda i: (0,) * a.ndim, pipeline_mode=pl.Buffered(1))


def _merge(x2d, ya, yf, yc, ys, gpre, wb, wg, bg, wo, gpost, *, tm):
    T, D = x2d.shape
    tm = min(tm, T)
    row = lambda w: pl.BlockSpec((tm, w), lambda i: (i, 0))
    return pl.pallas_call(
        _merge_kernel,
        grid=(T // tm,),
        in_specs=[row(D), row(BR_W), row(BR_W), row(BR_W), row(BR_W),
                  _const_spec(gpre), _const_spec(wb), _const_spec(wg), _const_spec(bg),
                  _const_spec(wo), _const_spec(gpost)],
        out_specs=row(D),
        out_shape=jax.ShapeDtypeStruct((T, D), F32),
        compiler_params=_cparams("parallel"),
        name="merge",
    )(x2d, ya, yf, yc, ys, gpre, wb, wg, bg, wo, gpost)


def _nt_dot(a, b):
    return lax.dot_general(a, b, (((1,), (1,)), ((), ())), preferred_element_type=F32)


def _mem_kv_kernel(m_ref, g_ref, w_ref, o_ref):
    m = _rms(m_ref[...], g_ref[...]).astype(BF16)
    o_ref[...] = jnp.dot(m, w_ref[...], preferred_element_type=F32).astype(BF16)


def _mem_kv(mem2d, g, w, *, tm):
    T, D = mem2d.shape
    tm = min(tm, T)
    return pl.pallas_call(
        _mem_kv_kernel,
        grid=(T // tm,),
        in_specs=[pl.BlockSpec((tm, D), lambda i: (i, 0)), _const_spec(g), _const_spec(w)],
        out_specs=pl.BlockSpec((tm, w.shape[1]), lambda i: (i, 0)),
        out_shape=jax.ShapeDtypeStruct((T, w.shape[1]), BF16),
        compiler_params=_cparams("parallel"),
        name="mem_kv",
    )(mem2d, g, w)


def _cross_attn_kernel(x_ref, kv_ref, gpre_ref, wq_ref, wo_ref, gpost_ref, o_ref, oh_ref):
    x = x_ref[...]
    hx = _rms(x, gpre_ref[...]).astype(BF16)
    q = (jnp.dot(hx, wq_ref[...], preferred_element_type=F32) * (X_HEAD_DIM ** -0.5)).astype(BF16)
    xw = X_HEADS * X_HEAD_DIM
    for h in range(X_HEADS):
        sl = slice(h * X_HEAD_DIM, (h + 1) * X_HEAD_DIM)
        s = _nt_dot(q[:, sl], kv_ref[:, sl])
        p = jnp.exp(s - jnp.max(s, axis=-1, keepdims=True))
        o = jnp.dot(p.astype(BF16), kv_ref[:, xw + h * X_HEAD_DIM:xw + (h + 1) * X_HEAD_DIM],
                    preferred_element_type=F32)
        oh_ref[:, sl] = (o / jnp.sum(p, axis=-1, keepdims=True)).astype(BF16)
    ox = jnp.dot(oh_ref[...], wo_ref[...], preferred_element_type=F32)
    o_ref[...] = x + _rms(ox, gpost_ref[...])


def _cross_attn(x2d, kv3, gpre, wq, wo, gpost, *, tm, seq):
    T, D = x2d.shape
    tm = min(tm, seq)
    per_b = seq // tm
    M, W = kv3.shape[1:]
    return pl.pallas_call(
        _cross_attn_kernel,
        grid=(T // tm,),
        in_specs=[pl.BlockSpec((tm, D), lambda i: (i, 0)),
                  pl.BlockSpec((None, M, W), lambda i: (i // per_b, 0, 0)),
                  _const_spec(gpre), _const_spec(wq), _const_spec(wo), _const_spec(gpost)],
        out_specs=pl.BlockSpec((tm, D), lambda i: (i, 0)),
        out_shape=jax.ShapeDtypeStruct((T, D), F32),
        scratch_shapes=[pltpu.VMEM((tm, X_HEADS * X_HEAD_DIM), BF16)],
        compiler_params=_cparams("parallel"),
        name="cross_attn",
    )(x2d, kv3, gpre, wq, wo, gpost)


def _ffn_kernel(x_ref, xh_ref, gpre_ref, wug_ref, wuv_ref, dwg_ref, dwv_ref, dbg_ref, dbv_ref,
                wd_ref, gpost_ref, o_ref, hf_ref, acc_ref, *, tm, per_b, nch):
    first = (pl.program_id(0) % per_b) == 0
    x = x_ref[...]
    hf_ref[pl.ds(0, FFN_HALO), :] = _rms(xh_ref[...], gpre_ref[...]).astype(BF16)
    hf_ref[pl.ds(FFN_HALO, tm), :] = _rms(x, gpre_ref[...]).astype(BF16)
    acc_ref[...] = jnp.zeros(acc_ref.shape, F32)
    live = (lax.broadcasted_iota(jnp.int32, (tm + FFN_HALO, 1), 0) >= FFN_HALO) | jnp.logical_not(first)

    def conv(up, dw, db):
        up = jnp.where(live, up, 0.0)
        u = dw[FFN_K - 1:FFN_K, :] * up[FFN_HALO:, :]
        for k in range(FFN_K - 1):
            sh = FFN_K - 1 - k
            u = u + dw[k:k + 1, :] * up[FFN_HALO - sh:FFN_HALO - sh + tm, :]
        return u + db

    def body(c, carry):
        hf = hf_ref[...]
        ug = conv(jnp.dot(hf, wug_ref[c], preferred_element_type=F32), dwg_ref[c], dbg_ref[c])
        uv = conv(jnp.dot(hf, wuv_ref[c], preferred_element_type=F32), dwv_ref[c], dbv_ref[c])
        act = (ug * jax.nn.sigmoid(ug) * uv).astype(BF16)
        acc_ref[...] += jnp.dot(act, wd_ref[c], preferred_element_type=F32)
        return carry

    lax.fori_loop(0, nch, body, 0)
    o_ref[...] = x + _rms(acc_ref[...], gpost_ref[...])


def _ffn(x2d, gpre, wug, wuv, dwg, dwv, dbg, dbv, wd, gpost, *, tm, seq):
    T, D = x2d.shape
    tm = min(tm, seq)
    per_b = seq // tm
    nch = wug.shape[0]
    hb = tm // FFN_HALO
    return pl.pallas_call(
        functools.partial(_ffn_kernel, tm=tm, per_b=per_b, nch=nch),
        grid=(T // tm,),
        in_specs=[pl.BlockSpec((tm, D), lambda i: (i, 0)),
                  pl.BlockSpec((FFN_HALO, D), lambda i: (jnp.maximum(i * hb - 1, 0), 0)),
                  _const_spec(gpre), _const_spec(wug), _const_spec(wuv), _const_spec(dwg),
                  _const_spec(dwv), _const_spec(dbg), _const_spec(dbv), _const_spec(wd),
                  _const_spec(gpost)],
        out_specs=pl.BlockSpec((tm, D), lambda i: (i, 0)),
        out_shape=jax.ShapeDtypeStruct((T, D), F32),
        scratch_shapes=[pltpu.VMEM((tm + FFN_HALO, D), BF16), pltpu.VMEM((tm, D), F32)],
        compiler_params=_cparams("parallel"),
        name="ffn",
    )(x2d, x2d, gpre, wug, wuv, dwg, dwv, dbg, dbv, wd, gpost)


FFN_CHUNK = 256
IN_MAIN = 11 * BR_W


def kernel(x, mem, norm_mix_pre, norm_mix_post, w_in, b_fgt, lam_q1, lam_k1, lam_q2, lam_k2, diff_norm, b_glu, conv_dw, conv_dw_b, conv_ln_g, conv_ln_b, sc_w, w_branch, w_gate, b_gate, w_out, norm_x_pre, norm_x_post, norm_mem, w_xq, w_xkv, w_xo, norm_ffn_pre, norm_ffn_post, w_up, ffn_dw, ffn_dw_b, w_down):
    B, S, D = x.shape
    depth = w_in.shape[0]
    T = B * S
    d_ff = w_down.shape[1]
    nch = d_ff // FFN_CHUNK
    fg_lo = 6 * BR_W
    fg_hi = fg_lo + F_HEADS
    tiles = S // LANES
    row = lambda a: a.reshape(1, -1)

    q_scale = (HEAD_DIM ** -0.5) * LOG2E
    col = jnp.arange(IN_MAIN) // BR_W
    col_scale = jnp.where((col == 0) | (col == 3), q_scale, 1.0).astype(F32).reshape(1, IN_MAIN)
    x2d = x.reshape(T, D)
    mem2d = mem.reshape(-1, D)
    for l in range(depth):
        lambda_init = 0.8 - 0.6 * math.exp(-0.3 * l)
        w_main = jnp.concatenate([w_in[l][:, :fg_lo], w_in[l][:, fg_hi:]], axis=1).astype(BF16)
        w_fg = jnp.pad(w_in[l][:, fg_lo:fg_hi], ((0, 0), (0, LANES - F_HEADS))).astype(BF16)
        wu = w_up[l].astype(BF16).reshape(D, 2, nch, FFN_CHUNK).transpose(1, 2, 0, 3)
        dw = ffn_dw[l].reshape(FFN_K, 2, nch, FFN_CHUNK).transpose(1, 2, 0, 3)
        db = ffn_dw_b[l].reshape(2, nch, 1, FFN_CHUNK)
        wd = w_down[l].astype(BF16).reshape(nch, FFN_CHUNK, D)

        y, fg = _in_proj(x2d, row(norm_mix_pre[l]), w_main, col_scale, w_fg, tm=1024, tn=512)
        y3 = y.reshape(B, S, -1)
        tr = lambda lo: y3[:, :, lo:lo + BR_W].transpose(0, 2, 1)
        ya = _diff_attn(tr(0), y3, tr(2 * BR_W), row(lam_q1[l]), row(lam_k1[l]), row(lam_q2[l]),
                        row(lam_k2[l]), diff_norm[l].reshape(-1, 1), blk=512, nh=2, lambda_init=lambda_init)
        fg_rows = fg.reshape(B, S, LANES)[:, :, :F_HEADS].transpose(0, 2, 1).reshape(B, F_HEADS * tiles, LANES)
        b_rows = jnp.repeat(b_fgt[l], tiles).reshape(F_HEADS * tiles, 1)
        c = _log_forget_cumsum(fg_rows, b_rows, tiles=tiles).reshape(B, F_HEADS, S)
        yf = _forget_attn(tr(3 * BR_W), y3, tr(5 * BR_W), c, blk=512, npair=2)
        yc, ys = _conv_branches(y3, row(b_glu[l]), conv_dw[l], row(conv_dw_b[l]), row(conv_ln_g[l]),
                                row(conv_ln_b[l]), sc_w[l], tc=512)
        x2d = _merge(x2d, ya.reshape(T, BR_W), yf.reshape(T, BR_W), yc.reshape(T, BR_W),
                     ys.reshape(T, BR_W), row(norm_mix_pre[l]), w_branch[l].astype(BF16),
                     w_gate[l].astype(BF16), row(b_gate[l]), w_out[l].astype(BF16),
                     row(norm_mix_post[l]), tm=512)

        kv = _mem_kv(mem2d, row(norm_mem[l]), w_xkv[l].astype(BF16), tm=256)
        x2d = _cross_attn(x2d, kv.reshape(B, -1, kv.shape[-1]), row(norm_x_pre[l]), w_xq[l].astype(BF16),
                          w_xo[l].astype(BF16), row(norm_x_post[l]), tm=512, seq=S)

        x2d = _ffn(x2d, row(norm_ffn_pre[l]), wu[0], wu[1], dw[0], dw[1], db[0], db[1], wd,
                   row(norm_ffn_post[l]), tm=512, seq=S)
    return x2d.reshape(B, S, D)
```

```python
import functools
import math

import jax
import jax.numpy as jnp
from jax import lax
from jax.experimental import pallas as pl
from jax.experimental.pallas import tpu as pltpu

F32 = jnp.float32
BF16 = jnp.bfloat16

CHUNK = 64
HEAD_DIM = 64
BR_W = 512
N_BRANCH = 4
F_HEADS = 8
CONV_K = 31
SC_K = 3
X_HEADS = 4
X_HEAD_DIM = 128
FFN_K = 3
EPS = 1e-6

LANES = 128
SUBLANES = 8
HALO = 32
FFN_HALO = 16
ONES_ROWS = 16
LOG2E = 1.4426950408889634
VMEM_LIMIT = 56 * 1024 * 1024


def _cparams(*sem):
    return pltpu.CompilerParams(dimension_semantics=sem, vmem_limit_bytes=VMEM_LIMIT)


def _rms(x, g):
    return x * lax.rsqrt(jnp.mean(x * x, axis=-1, keepdims=True) + EPS) * g


def _nt_dot(a, b):
    return lax.dot_general(a, b, (((1,), (1,)), ((), ())), preferred_element_type=F32)


def _const_spec(a):
    return pl.BlockSpec(a.shape, lambda i: (0,) * a.ndim, pipeline_mode=pl.Buffered(1))


def _in_proj_kernel(x_ref, g_ref, wn_ref, wt_ref, wfg_ref, yn_ref, yt_ref, fg_ref, *, q_scale):
    h = _rms(x_ref[...], g_ref[...]).astype(BF16)
    fg_ref[...] = jnp.dot(h, wfg_ref[...], preferred_element_type=F32)
    for c in range(wn_ref.shape[1] // BR_W):
        sl = slice(c * BR_W, (c + 1) * BR_W)
        yn_ref[:, sl] = jnp.dot(h, wn_ref[:, sl], preferred_element_type=F32).astype(BF16)
    for c in range(wt_ref.shape[0] // BR_W):
        sl = slice(c * BR_W, (c + 1) * BR_W)
        r = _nt_dot(wt_ref[sl, :], h)
        if c % 2 == 0:
            r = r * q_scale
        yt_ref[sl, :] = r.astype(BF16)


def _in_proj(x2d, g, wn, wt, wfg, *, tm, seq, q_scale):
    T, D = x2d.shape
    tm = min(tm, seq)
    per_b = seq // tm
    return pl.pallas_call(
        functools.partial(_in_proj_kernel, q_scale=q_scale),
        grid=(T // tm,),
        in_specs=[pl.BlockSpec((tm, D), lambda i: (i, 0)),
                  _const_spec(g), _const_spec(wn), _const_spec(wt), _const_spec(wfg)],
        out_specs=[pl.BlockSpec((tm, wn.shape[1]), lambda i: (i, 0)),
                   pl.BlockSpec((None, wt.shape[0], tm), lambda i: (i // per_b, 0, i % per_b)),
                   pl.BlockSpec((tm, LANES), lambda i: (i, 0))],
        out_shape=[jax.ShapeDtypeStruct((T, wn.shape[1]), BF16),
                   jax.ShapeDtypeStruct((T // seq, wt.shape[0], seq), BF16),
                   jax.ShapeDtypeStruct((T, LANES), F32)],
        compiler_params=_cparams("parallel"),
        name="in_proj",
    )(x2d, g, wn, wt, wfg)


def _split_rows(qT):
    row = lax.broadcasted_iota(jnp.int32, qT.shape, 0)
    zero = jnp.zeros_like(qT)
    return jnp.where(row < HEAD_DIM, qT, zero), jnp.where(row >= HEAD_DIM, qT, zero)


def _online_softmax_step(zT, m_ref, acc_ref, idx, vT_aug, shift=None):
    m_prev = m_ref[idx]
    z_max = jnp.max(zT, axis=0, keepdims=True)
    m_new = jnp.maximum(m_prev, z_max if shift is None else z_max + shift)
    alpha = jnp.exp2(m_prev - m_new)
    pT = jnp.exp2(zT - (m_new if shift is None else m_new - shift))
    acc_ref[idx] = alpha * acc_ref[idx] + jnp.dot(vT_aug, pT.astype(BF16), preferred_element_type=F32)
    m_ref[idx] = m_new


def _diff_attn_kernel(qT_ref, k_ref, vT_ref, lq1_ref, lk1_ref, lq2_ref, lk2_ref, gn_ref, o_ref,
                      m_ref, acc_ref, *, blk, nh, lambda_init):
    qi = pl.program_id(2)
    qTc = []
    for h in range(nh):
        qTc += _split_rows(qT_ref[h * LANES:(h + 1) * LANES, :])
    m_ref[...] = jnp.full(m_ref.shape, -jnp.inf, F32)
    acc_ref[...] = jnp.zeros(acc_ref.shape, F32)
    ones = jnp.ones((ONES_ROWS, blk), BF16)

    def block(j, masked):
        start = pl.multiple_of(j * blk, blk)
        ks = [k_ref[pl.ds(start, blk), h * LANES:(h + 1) * LANES] for h in range(nh)]
        vTs = [jnp.concatenate([vT_ref[h * LANES:(h + 1) * LANES, pl.ds(start, blk)], ones], axis=0)
               for h in range(nh)]
        sTs = [jnp.dot(ks[c // 2], qTc[c], preferred_element_type=F32) for c in range(2 * nh)]
        for c in range(2 * nh):
            sT = sTs[c]
            if masked:
                kc = lax.broadcasted_iota(jnp.int32, (blk, blk), 0) // CHUNK
                qc = lax.broadcasted_iota(jnp.int32, (blk, blk), 1) // CHUNK
                sT = jnp.where(kc <= qc, sT, -jnp.inf)
            _online_softmax_step(sT, m_ref, acc_ref, c, vTs[c // 2])

    def body(j, carry):
        block(j, False)
        return carry

    lax.fori_loop(0, qi, body, 0)
    block(qi, True)

    lam = (jnp.exp(jnp.sum(lq1_ref[...] * lk1_ref[...], axis=-1, keepdims=True))
           - jnp.exp(jnp.sum(lq2_ref[...] * lk2_ref[...], axis=-1, keepdims=True)) + lambda_init)
    for h in range(nh):
        a0, a1 = acc_ref[2 * h], acc_ref[2 * h + 1]
        oT = a0[:LANES] / a0[LANES:LANES + 1] - lam * (a1[:LANES] / a1[LANES:LANES + 1])
        yT = oT * lax.rsqrt(jnp.mean(oT * oT, axis=0, keepdims=True) + EPS) * gn_ref[...]
        o_ref[:, h * LANES:(h + 1) * LANES] = (yT * (1.0 - lambda_init)).T.astype(BF16)


def _diff_attn(yT, yn, lq1, lk1, lq2, lk2, gn_col, *, blk, nh, lambda_init):
    B, S, _ = yn.shape
    blk = min(blk, S)
    W = nh * LANES
    groups = BR_W // W
    vec = pl.BlockSpec((1, HEAD_DIM), lambda b, g, i: (0, 0))
    return pl.pallas_call(
        functools.partial(_diff_attn_kernel, blk=blk, nh=nh, lambda_init=lambda_init),
        grid=(B, groups, S // blk),
        in_specs=[pl.BlockSpec((None, W, blk), lambda b, g, i: (b, g, i)),
                  pl.BlockSpec((None, S, W), lambda b, g, i: (b, 0, g)),
                  pl.BlockSpec((None, W, S), lambda b, g, i: (b, groups + g, 0)),
                  vec, vec, vec, vec,
                  pl.BlockSpec((LANES, 1), lambda b, g, i: (0, 0))],
        out_specs=pl.BlockSpec((None, blk, W), lambda b, g, i: (b, i, g)),
        out_shape=jax.ShapeDtypeStruct((B, S, BR_W), BF16),
        scratch_shapes=[pltpu.VMEM((2 * nh, 1, blk), F32),
                        pltpu.VMEM((2 * nh, LANES + ONES_ROWS, blk), F32)],
        compiler_params=_cparams("parallel", "parallel", "arbitrary"),
        name="diff_attn",
    )(yT, yn, yT, lq1, lk1, lq2, lk2, gn_col)


def _log_forget_cumsum_kernel(fg_ref, b_ref, c_ref, *, tiles):
    x = fg_ref[...] + b_ref[...]
    c = jnp.minimum(x, 0.0) - jnp.log(1.0 + jnp.exp(-jnp.abs(x)))
    rows = c.shape[0]
    lane = lax.broadcasted_iota(jnp.int32, (rows, LANES), 1)
    d = 1
    while d < LANES:
        c = c + jnp.where(lane >= d, pltpu.roll(c, d, 1), 0.0)
        d *= 2
    tile = lax.broadcasted_iota(jnp.int32, (rows, LANES), 0) % tiles
    tot = jnp.broadcast_to(c[:, LANES - 1:LANES], (rows, LANES))
    off = jnp.where(tile >= 1, pltpu.roll(tot, 1, 0), 0.0)
    d = 1
    while d < tiles:
        off = off + jnp.where(tile >= d, pltpu.roll(off, d, 0), 0.0)
        d *= 2
    c_ref[...] = (c + off) * LOG2E


def _log_forget_cumsum(fg_rows, b_rows, *, tiles):
    B, R, _ = fg_rows.shape
    return pl.pallas_call(
        functools.partial(_log_forget_cumsum_kernel, tiles=tiles),
        grid=(B,),
        in_specs=[pl.BlockSpec((None, R, LANES), lambda b: (b, 0, 0)),
                  pl.BlockSpec((R, 1), lambda b: (0, 0))],
        out_specs=pl.BlockSpec((None, R, LANES), lambda b: (b, 0, 0)),
        out_shape=jax.ShapeDtypeStruct((B, R, LANES), F32),
        compiler_params=_cparams("parallel"),
        name="log_forget_cumsum",
    )(fg_rows, b_rows)


def _forget_attn_kernel(qT_ref, k_ref, vT_ref, c_ref, o_ref, m_ref, acc_ref, ck_ref, *, blk, npair):
    qi = pl.program_id(2)
    nhead = 2 * npair
    S = c_ref.shape[-1]

    @pl.when(qi == 0)
    def _():
        for h in range(nhead):
            ck_ref[h] = jnp.broadcast_to(c_ref[h:h + 1, :], (LANES, S)).T

    qTh = []
    for p in range(npair):
        qTh += _split_rows(qT_ref[p * LANES:(p + 1) * LANES, :])
    q0 = pl.multiple_of(qi * blk, blk)
    cq = [c_ref[h:h + 1, pl.ds(q0, blk)] for h in range(nhead)]
    m_ref[...] = jnp.full(m_ref.shape, -jnp.inf, F32)
    acc_ref[...] = jnp.zeros(acc_ref.shape, F32)
    ones = jnp.ones((ONES_ROWS, blk), BF16)

    def block(j, masked):
        start = pl.multiple_of(j * blk, blk)
        ks = [k_ref[pl.ds(start, blk), p * LANES:(p + 1) * LANES] for p in range(npair)]
        vTs = [jnp.concatenate([vT_ref[h * HEAD_DIM:(h + 1) * HEAD_DIM, pl.ds(start, blk)], ones], axis=0)
               for h in range(nhead)]
        sTs = [jnp.dot(ks[h // 2], qTh[h], preferred_element_type=F32) for h in range(nhead)]
        for h in range(nhead):
            ck = ck_ref[h, pl.ds(start, blk), :]
            zT = sTs[h] - jnp.concatenate([ck] * (blk // LANES), axis=1)
            if masked:
                kk = lax.broadcasted_iota(jnp.int32, (blk, blk), 0)
                qq = lax.broadcasted_iota(jnp.int32, (blk, blk), 1)
                zT = jnp.where(kk <= qq, zT, -jnp.inf)
            _online_softmax_step(zT, m_ref, acc_ref, h, vTs[h], shift=cq[h])

    def body(j, carry):
        block(j, False)
        return carry

    lax.fori_loop(0, qi, body, 0)
    block(qi, True)

    for p in range(npair):
        a0, a1 = acc_ref[2 * p], acc_ref[2 * p + 1]
        oT = jnp.concatenate([a0[:HEAD_DIM] / a0[HEAD_DIM:HEAD_DIM + 1],
                              a1[:HEAD_DIM] / a1[HEAD_DIM:HEAD_DIM + 1]], axis=0)
        o_ref[:, p * LANES:(p + 1) * LANES] = oT.T.astype(BF16)


def _forget_attn(yT, yn, c, *, blk, npair):
    B, S, _ = yn.shape
    blk = min(blk, S)
    W = npair * LANES
    groups = BR_W // W
    c4 = c.reshape(B, groups, 2 * npair, S)
    return pl.pallas_call(
        functools.partial(_forget_attn_kernel, blk=blk, npair=npair),
        grid=(B, groups, S // blk),
        in_specs=[pl.BlockSpec((None, W, blk), lambda b, g, i: (b, 2 * groups + g, i)),
                  pl.BlockSpec((None, S, W), lambda b, g, i: (b, 0, groups + g)),
                  pl.BlockSpec((None, W, S), lambda b, g, i: (b, 3 * groups + g, 0)),
                  pl.BlockSpec((None, None, 2 * npair, S), lambda b, g, i: (b, g, 0, 0))],
        out_specs=pl.BlockSpec((None, blk, W), lambda b, g, i: (b, i, g)),
        out_shape=jax.ShapeDtypeStruct((B, S, BR_W), BF16),
        scratch_shapes=[pltpu.VMEM((2 * npair, 1, blk), F32),
                        pltpu.VMEM((2 * npair, HEAD_DIM + ONES_ROWS, blk), F32),
                        pltpu.VMEM((2 * npair, S, LANES), F32)],
        compiler_params=_cparams("parallel", "arbitrary", "arbitrary"),
        name="forget_attn",
    )(yT, yn, yT, c4)


def _conv_branches_kernel(cu_ref, cuh_ref, sx_ref, sxh_ref, sb_ref, sc_ref, sch_ref,
                          bglu_ref, cw_ref, cb_ref, lng_ref, lnb_ref, scw_ref,
                          yc_ref, ys_ref, g_ref, gs_ref, p_ref, cv_ref, *, tc, rows):
    first = pl.program_id(1) == 0
    n = tc + HALO

    def glu(u):
        u = u.astype(F32) + bglu_ref[...]
        return u[:, :BR_W] * jax.nn.sigmoid(u[:, BR_W:])

    g_ref[pl.ds(0, HALO), :] = jnp.where(first, 0.0, glu(cuh_ref[...]))
    g_ref[pl.ds(HALO, tc), :] = glu(cu_ref[...])
    g_ref[pl.ds(n, SUBLANES), :] = jnp.zeros((SUBLANES, BR_W), F32)
    for b in range(SUBLANES):
        gs_ref[b] = g_ref[pl.ds(b, n), :]

    def body(r, carry):
        r0 = pl.multiple_of(r * rows, rows)
        acc = jnp.zeros((rows, BR_W), F32)
        for k in range(CONV_K):
            a, b = divmod(HALO - (CONV_K - 1) + k, SUBLANES)
            start = pl.multiple_of(r0 + a * SUBLANES, SUBLANES)
            acc = acc + cw_ref[k:k + 1, :] * gs_ref[b, pl.ds(start, rows), :]
        cv_ref[pl.ds(r0, rows), :] = acc
        return carry

    lax.fori_loop(0, tc // rows, body, 0)
    cv = cv_ref[...] + cb_ref[...]
    mu = jnp.mean(cv, axis=-1, keepdims=True)
    var = jnp.mean(jnp.square(cv - mu), axis=-1, keepdims=True)
    yn = (cv - mu) * lax.rsqrt(var + EPS) * lng_ref[...] + lnb_ref[...]
    yc_ref[...] = (yn * jax.nn.sigmoid(yn)).astype(BF16)

    ph = sch_ref[...].astype(F32) * sxh_ref[...].astype(F32)
    p_ref[pl.ds(0, HALO), :] = jnp.where(first, 0.0, ph)
    p_ref[pl.ds(HALO, tc), :] = sc_ref[...].astype(F32) * sx_ref[...].astype(F32)
    sacc = scw_ref[0:1, :] * p_ref[pl.ds(HALO - (SC_K - 1), tc), :]
    for k in range(1, SC_K):
        sacc = sacc + scw_ref[k:k + 1, :] * p_ref[pl.ds(HALO - (SC_K - 1) + k, tc), :]
    ys_ref[...] = (sb_ref[...].astype(F32) * sacc).astype(BF16)


def _conv_branches(y3, bglu, cw, cb, lng, lnb, scw, *, tc, rows=32):
    B, S, _ = y3.shape
    tc = min(tc, S)
    cu_blk = 1
    sx_blk = 4
    hb = tc // HALO

    def cur(cb_):
        return lambda b, i: (b, i, cb_)

    def halo(cb_):
        return lambda b, i: (b, jnp.maximum(i * hb - 1, 0), cb_)

    def full(a):
        return pl.BlockSpec(a.shape, lambda b, i: (0,) * a.ndim)

    out_spec = pl.BlockSpec((None, tc, BR_W), lambda b, i: (b, i, 0))
    return pl.pallas_call(
        functools.partial(_conv_branches_kernel, tc=tc, rows=rows),
        grid=(B, S // tc),
        in_specs=[pl.BlockSpec((None, tc, 2 * BR_W), cur(cu_blk)),
                  pl.BlockSpec((None, HALO, 2 * BR_W), halo(cu_blk)),
                  pl.BlockSpec((None, tc, BR_W), cur(sx_blk)),
                  pl.BlockSpec((None, HALO, BR_W), halo(sx_blk)),
                  pl.BlockSpec((None, tc, BR_W), cur(sx_blk + 1)),
                  pl.BlockSpec((None, tc, BR_W), cur(sx_blk + 2)),
                  pl.BlockSpec((None, HALO, BR_W), halo(sx_blk + 2)),
                  full(bglu), full(cw), full(cb), full(lng), full(lnb), full(scw)],
        out_specs=[out_spec, out_spec],
        out_shape=[jax.ShapeDtypeStruct((B, S, BR_W), BF16)] * 2,
        scratch_shapes=[pltpu.VMEM((tc + HALO + SUBLANES, BR_W), F32),
                        pltpu.VMEM((SUBLANES, tc + HALO, BR_W), F32),
                        pltpu.VMEM((tc + HALO, BR_W), F32),
                        pltpu.VMEM((tc, BR_W), F32)],
        compiler_params=_cparams("parallel", "arbitrary"),
        name="conv_branches",
    )(y3, y3, y3, y3, y3, y3, y3, bglu, cw, cb, lng, lnb, scw)


def _merge_kernel(x_ref, ya_ref, yf_ref, yc_ref, ys_ref, gpre_ref, wb_ref, wg_ref, bg_ref, wo_ref,
                  gpost_ref, o_ref):
    x = x_ref[...]
    D = x.shape[-1]
    h = _rms(x, gpre_ref[...]).astype(BF16)
    mixed = None
    for n, br_ref in enumerate((ya_ref, yf_ref, yc_ref, ys_ref)):
        proj = jnp.dot(br_ref[...], wb_ref[n], preferred_element_type=F32)
        gate = jax.nn.sigmoid(jnp.dot(h, wg_ref[:, n * D:(n + 1) * D], preferred_element_type=F32)
                              + bg_ref[:, n * D:(n + 1) * D])
        mixed = gate * proj if mixed is None else mixed + gate * proj
    out = jnp.dot(mixed.astype(BF16), wo_ref[...], preferred_element_type=F32)
    o_ref[...] = x + _rms(out, gpost_ref[...])


def _merge(x2d, ya, yf, yc, ys, gpre, wb, wg, bg, wo, gpost, *, tm):
    T, D = x2d.shape
    tm = min(tm, T)
    row = lambda w: pl.BlockSpec((tm, w), lambda i: (i, 0))
    return pl.pallas_call(
        _merge_kernel,
        grid=(T // tm,),
        in_specs=[row(D), row(BR_W), row(BR_W), row(BR_W), row(BR_W),
                  _const_spec(gpre), _const_spec(wb), _const_spec(wg), _const_spec(bg),
                  _const_spec(wo), _const_spec(gpost)],
        out_specs=row(D),
        out_shape=jax.ShapeDtypeStruct((T, D), F32),
        compiler_params=_cparams("parallel"),
        name="merge",
    )(x2d, ya, yf, yc, ys, gpre, wb, wg, bg, wo, gpost)


def _mem_kv_kernel(m_ref, g_ref, w_ref, o_ref):
    m = _rms(m_ref[...], g_ref[...]).astype(BF16)
    o_ref[...] = jnp.dot(m, w_ref[...], preferred_element_type=F32).astype(BF16)


def _mem_kv(mem2d, g, w, *, tm):
    T, D = mem2d.shape
    tm = min(tm, T)
    return pl.pallas_call(
        _mem_kv_kernel,
        grid=(T // tm,),
        in_specs=[pl.BlockSpec((tm, D), lambda i: (i, 0)), _const_spec(g), _const_spec(w)],
        out_specs=pl.BlockSpec((tm, w.shape[1]), lambda i: (i, 0)),
        out_shape=jax.ShapeDtypeStruct((T, w.shape[1]), BF16),
        compiler_params=_cparams("parallel"),
        name="mem_kv",
    )(mem2d, g, w)


def _cross_attn_kernel(x_ref, kv_ref, gpre_ref, wq_ref, wo_ref, gpost_ref, o_ref, oh_ref):
    x = x_ref[...]
    hx = _rms(x, gpre_ref[...]).astype(BF16)
    q = (jnp.dot(hx, wq_ref[...], preferred_element_type=F32) * (X_HEAD_DIM ** -0.5)).astype(BF16)
    xw = X_HEADS * X_HEAD_DIM
    for h in range(X_HEADS):
        sl = slice(h * X_HEAD_DIM, (h + 1) * X_HEAD_DIM)
        s = _nt_dot(q[:, sl], kv_ref[:, sl])
        p = jnp.exp(s - jnp.max(s, axis=-1, keepdims=True))
        o = jnp.dot(p.astype(BF16), kv_ref[:, xw + h * X_HEAD_DIM:xw + (h + 1) * X_HEAD_DIM],
                    preferred_element_type=F32)
        oh_ref[:, sl] = (o / jnp.sum(p, axis=-1, keepdims=True)).astype(BF16)
    ox = jnp.dot(oh_ref[...], wo_ref[...], preferred_element_type=F32)
    o_ref[...] = x + _rms(ox, gpost_ref[...])


def _cross_attn(x2d, kv3, gpre, wq, wo, gpost, *, tm, seq):
    T, D = x2d.shape
    tm = min(tm, seq)
    per_b = seq // tm
    M, W = kv3.shape[1:]
    return pl.pallas_call(
        _cross_attn_kernel,
        grid=(T // tm,),
        in_specs=[pl.BlockSpec((tm, D), lambda i: (i, 0)),
                  pl.BlockSpec((None, M, W), lambda i: (i // per_b, 0, 0)),
                  _const_spec(gpre), _const_spec(wq), _const_spec(wo), _const_spec(gpost)],
        out_specs=pl.BlockSpec((tm, D), lambda i: (i, 0)),
        out_shape=jax.ShapeDtypeStruct((T, D), F32),
        scratch_shapes=[pltpu.VMEM((tm, X_HEADS * X_HEAD_DIM), BF16)],
        compiler_params=_cparams("parallel"),
        name="cross_attn",
    )(x2d, kv3, gpre, wq, wo, gpost)


FFN_CHUNK = 256


def _ffn_kernel(x_ref, xh_ref, gpre_ref, wu_ref, dw_ref, db_ref, wd_ref, gpost_ref, o_ref, hf_ref,
                *, tm, per_b):
    first = (pl.program_id(0) % per_b) == 0
    d_ff = wd_ref.shape[0]
    x = x_ref[...]
    hf_ref[pl.ds(0, FFN_HALO), :] = jnp.where(first, 0.0, _rms(xh_ref[...], gpre_ref[...])).astype(BF16)
    hf_ref[pl.ds(FFN_HALO, tm), :] = _rms(x, gpre_ref[...]).astype(BF16)

    def cols(c, half):
        return slice(half * d_ff + c * FFN_CHUNK, half * d_ff + (c + 1) * FFN_CHUNK)

    def up_proj(c):
        hf = hf_ref[...]
        return tuple(jnp.dot(hf, wu_ref[:, cols(c, half)], preferred_element_type=F32) for half in range(2))

    def conv(up, sl):
        u = dw_ref[FFN_K - 1:FFN_K, sl] * up[FFN_HALO:, :]
        for k in range(FFN_K - 1):
            sh = FFN_K - 1 - k
            u = u + dw_ref[k:k + 1, sl] * up[FFN_HALO - sh:FFN_HALO - sh + tm, :]
        return u + db_ref[:, sl]

    nch = d_ff // FFN_CHUNK
    acts = []
    ups = up_proj(0)
    for c in range(nch):
        nxt = up_proj(c + 1) if c + 1 < nch else None
        ug = conv(ups[0], cols(c, 0))
        uv = conv(ups[1], cols(c, 1))
        acts.append((ug * jax.nn.sigmoid(ug) * uv).astype(BF16))
        ups = nxt
    out = jnp.dot(jnp.concatenate(acts, axis=1), wd_ref[...], preferred_element_type=F32)
    o_ref[...] = x + _rms(out, gpost_ref[...])


def _ffn(x2d, gpre, wu, dw, db, wd, gpost, *, tm, seq):
    T, D = x2d.shape
    tm = min(tm, seq)
    per_b = seq // tm
    hb = tm // FFN_HALO
    return pl.pallas_call(
        functools.partial(_ffn_kernel, tm=tm, per_b=per_b),
        grid=(T // tm,),
        in_specs=[pl.BlockSpec((tm, D), lambda i: (i, 0)),
                  pl.BlockSpec((FFN_HALO, D), lambda i: (jnp.maximum(i * hb - 1, 0), 0)),
                  _const_spec(gpre), _const_spec(wu), _const_spec(dw), _const_spec(db), _const_spec(wd),
                  _const_spec(gpost)],
        out_specs=pl.BlockSpec((tm, D), lambda i: (i, 0)),
        out_shape=jax.ShapeDtypeStruct((T, D), F32),
        scratch_shapes=[pltpu.VMEM((tm + FFN_HALO, D), BF16)],
        compiler_params=_cparams("parallel"),
        name="ffn",
    )(x2d, x2d, gpre, wu, dw, db, wd, gpost)


def kernel(x, mem, norm_mix_pre, norm_mix_post, w_in, b_fgt, lam_q1, lam_k1, lam_q2, lam_k2, diff_norm, b_glu, conv_dw, conv_dw_b, conv_ln_g, conv_ln_b, sc_w, w_branch, w_gate, b_gate, w_out, norm_x_pre, norm_x_post, norm_mem, w_xq, w_xkv, w_xo, norm_ffn_pre, norm_ffn_post, w_up, ffn_dw, ffn_dw_b, w_down):
    B, S, D = x.shape
    depth = w_in.shape[0]
    T = B * S
    fg_lo = 6 * BR_W
    fg_hi = fg_lo + F_HEADS
    tiles = S // LANES
    row = lambda a: a.reshape(1, -1)

    q_scale = (HEAD_DIM ** -0.5) * LOG2E
    x2d = x.reshape(T, D)
    mem2d = mem.reshape(-1, D)
    for l in range(depth):
        lambda_init = 0.8 - 0.6 * math.exp(-0.3 * l)
        wl = w_in[l]
        seg = lambda i: wl[:, i * BR_W:(i + 1) * BR_W]
        w_nat = jnp.concatenate([seg(1), seg(4), wl[:, fg_hi:]], axis=1).astype(BF16)
        w_tr = jnp.concatenate([seg(0), seg(2), seg(3), seg(5)], axis=1).T.astype(BF16)
        w_fg = jnp.pad(wl[:, fg_lo:fg_hi], ((0, 0), (0, LANES - F_HEADS))).astype(BF16)

        yn, yT, fg = _in_proj(x2d, row(norm_mix_pre[l]), w_nat, w_tr, w_fg, tm=512, seq=S, q_scale=q_scale)
        y3 = yn.reshape(B, S, -1)
        ya = _diff_attn(yT, y3, row(lam_q1[l]), row(lam_k1[l]), row(lam_q2[l]), row(lam_k2[l]),
                        diff_norm[l].reshape(-1, 1), blk=512, nh=2, lambda_init=lambda_init)
        fg_rows = fg.reshape(B, S, LANES)[:, :, :F_HEADS].transpose(0, 2, 1).reshape(B, F_HEADS * tiles, LANES)
        b_rows = jnp.repeat(b_fgt[l], tiles).reshape(F_HEADS * tiles, 1)
        c = _log_forget_cumsum(fg_rows, b_rows, tiles=tiles).reshape(B, F_HEADS, S)
        yf = _forget_attn(yT, y3, c, blk=512, npair=2)
        yc, ys = _conv_branches(y3, row(b_glu[l]), conv_dw[l], row(conv_dw_b[l]), row(conv_ln_g[l]),
                                row(conv_ln_b[l]), sc_w[l], tc=512)
        x2d = _merge(x2d, ya.reshape(T, BR_W), yf.reshape(T, BR_W), yc.reshape(T, BR_W),
                     ys.reshape(T, BR_W), row(norm_mix_pre[l]), w_branch[l].astype(BF16),
                     w_gate[l].astype(BF16), row(b_gate[l]), w_out[l].astype(BF16),
                     row(norm_mix_post[l]), tm=512)

        kv = _mem_kv(mem2d, row(norm_mem[l]), w_xkv[l].astype(BF16), tm=256)
        x2d = _cross_attn(x2d, kv.reshape(B, -1, kv.shape[-1]), row(norm_x_pre[l]), w_xq[l].astype(BF16),
                          w_xo[l].astype(BF16), row(norm_x_post[l]), tm=512, seq=S)

        x2d = _ffn(x2d, row(norm_ffn_pre[l]), w_up[l].astype(BF16), ffn_dw[l], row(ffn_dw_b[l]),
                   w_down[l].astype(BF16), row(norm_ffn_post[l]), tm=1024, seq=S)
    return x2d.reshape(B, S, D)
```

```python
import functools
import math

import jax
import jax.numpy as jnp
from jax import lax
from jax.experimental import pallas as pl
from jax.experimental.pallas import tpu as pltpu

F32 = jnp.float32
BF16 = jnp.bfloat16

CHUNK = 64
HEAD_DIM = 64
BR_W = 512
N_BRANCH = 4
F_HEADS = 8
CONV_K = 31
SC_K = 3
X_HEADS = 4
X_HEAD_DIM = 128
FFN_K = 3
EPS = 1e-6

LANES = 128
SUBLANES = 8
HALO = 32
FFN_HALO = 16
ONES_ROWS = 16
LOG2E = 1.4426950408889634
VMEM_LIMIT = 56 * 1024 * 1024


def _cparams(*sem):
    return pltpu.CompilerParams(dimension_semantics=sem, vmem_limit_bytes=VMEM_LIMIT)


def _rms(x, g):
    return x * lax.rsqrt(jnp.mean(x * x, axis=-1, keepdims=True) + EPS) * g


def _nt_dot(a, b):
    return lax.dot_general(a, b, (((1,), (1,)), ((), ())), preferred_element_type=F32)


def _const_spec(a):
    return pl.BlockSpec(a.shape, lambda i: (0,) * a.ndim, pipeline_mode=pl.Buffered(1))


def _in_proj_kernel(x_ref, g_ref, wn_ref, wt_ref, wfg_ref, yn_ref, yt_ref, fg_ref, *, q_scale):
    h = _rms(x_ref[...], g_ref[...]).astype(BF16)
    fg_ref[...] = jnp.dot(h, wfg_ref[...], preferred_element_type=F32)
    for c in range(wn_ref.shape[1] // BR_W):
        sl = slice(c * BR_W, (c + 1) * BR_W)
        yn_ref[:, sl] = jnp.dot(h, wn_ref[:, sl], preferred_element_type=F32).astype(BF16)
    for c in range(wt_ref.shape[0] // BR_W):
        sl = slice(c * BR_W, (c + 1) * BR_W)
        r = _nt_dot(wt_ref[sl, :], h)
        if c % 2 == 0:
            r = r * q_scale
        yt_ref[sl, :] = r.astype(BF16)


def _in_proj(x2d, g, wn, wt, wfg, *, tm, seq, q_scale):
    T, D = x2d.shape
    tm = min(tm, seq)
    per_b = seq // tm
    return pl.pallas_call(
        functools.partial(_in_proj_kernel, q_scale=q_scale),
        grid=(T // tm,),
        in_specs=[pl.BlockSpec((tm, D), lambda i: (i, 0)),
                  _const_spec(g), _const_spec(wn), _const_spec(wt), _const_spec(wfg)],
        out_specs=[pl.BlockSpec((tm, wn.shape[1]), lambda i: (i, 0)),
                   pl.BlockSpec((None, wt.shape[0], tm), lambda i: (i // per_b, 0, i % per_b)),
                   pl.BlockSpec((tm, LANES), lambda i: (i, 0))],
        out_shape=[jax.ShapeDtypeStruct((T, wn.shape[1]), BF16),
                   jax.ShapeDtypeStruct((T // seq, wt.shape[0], seq), BF16),
                   jax.ShapeDtypeStruct((T, LANES), F32)],
        compiler_params=_cparams("parallel"),
        name="in_proj",
    )(x2d, g, wn, wt, wfg)


def _split_rows(qT):
    row = lax.broadcasted_iota(jnp.int32, qT.shape, 0)
    zero = jnp.zeros_like(qT)
    return jnp.where(row < HEAD_DIM, qT, zero), jnp.where(row >= HEAD_DIM, qT, zero)


def _online_softmax_step(zT, m_ref, acc_ref, idx, qs, vT_aug, shift=None):
    m_prev = m_ref[idx, :, qs]
    z_max = jnp.max(zT, axis=0, keepdims=True)
    m_new = jnp.maximum(m_prev, z_max if shift is None else z_max + shift)
    alpha = jnp.exp2(m_prev - m_new)
    pT = jnp.exp2(zT - (m_new if shift is None else m_new - shift))
    acc_ref[idx, :, qs] = (alpha * acc_ref[idx, :, qs]
                           + jnp.dot(vT_aug, pT.astype(BF16), preferred_element_type=F32))
    m_ref[idx, :, qs] = m_new


def _diagonal_parts(blk):
    half = blk // 2
    return ((0, half, 0), (half, half, half))


def _diff_attn_kernel(qT_ref, k_ref, vT_ref, lq1_ref, lk1_ref, lq2_ref, lk2_ref, gn_ref, o_ref,
                      m_ref, acc_ref, *, blk, nh, lambda_init):
    qi = pl.program_id(2)
    qTc = []
    for h in range(nh):
        qTc += _split_rows(qT_ref[h * LANES:(h + 1) * LANES, :])
    m_ref[...] = jnp.full(m_ref.shape, -jnp.inf, F32)
    acc_ref[...] = jnp.zeros(acc_ref.shape, F32)
    def block(start, nk, q_lo, masked):
        qs = slice(q_lo, blk)
        ones = jnp.ones((ONES_ROWS, nk), BF16)
        ks = [k_ref[pl.ds(start, nk), h * LANES:(h + 1) * LANES] for h in range(nh)]
        vTs = [jnp.concatenate([vT_ref[h * LANES:(h + 1) * LANES, pl.ds(start, nk)], ones], axis=0)
               for h in range(nh)]
        sTs = [jnp.dot(ks[c // 2], qTc[c][:, qs], preferred_element_type=F32) for c in range(2 * nh)]
        for c in range(2 * nh):
            sT = sTs[c]
            if masked:
                kc = lax.broadcasted_iota(jnp.int32, sT.shape, 0) // CHUNK
                qc = lax.broadcasted_iota(jnp.int32, sT.shape, 1) // CHUNK
                sT = jnp.where(kc <= qc, sT, -jnp.inf)
            _online_softmax_step(sT, m_ref, acc_ref, c, qs, vTs[c // 2])

    def body(j, carry):
        block(pl.multiple_of(j * blk, blk), blk, 0, False)
        return carry

    lax.fori_loop(0, qi, body, 0)
    for k_lo, nk, q_lo in _diagonal_parts(blk):
        block(pl.multiple_of(qi * blk + k_lo, nk), nk, q_lo, True)

    lam = (jnp.exp(jnp.sum(lq1_ref[...] * lk1_ref[...], axis=-1, keepdims=True))
           - jnp.exp(jnp.sum(lq2_ref[...] * lk2_ref[...], axis=-1, keepdims=True)) + lambda_init)
    for h in range(nh):
        a0, a1 = acc_ref[2 * h], acc_ref[2 * h + 1]
        oT = a0[:LANES] / a0[LANES:LANES + 1] - lam * (a1[:LANES] / a1[LANES:LANES + 1])
        yT = oT * lax.rsqrt(jnp.mean(oT * oT, axis=0, keepdims=True) + EPS) * gn_ref[...]
        o_ref[:, h * LANES:(h + 1) * LANES] = (yT * (1.0 - lambda_init)).T.astype(BF16)


def _diff_attn(yT, yn, lq1, lk1, lq2, lk2, gn_col, *, blk, nh, lambda_init):
    B, S, _ = yn.shape
    blk = min(blk, S)
    W = nh * LANES
    groups = BR_W // W
    vec = pl.BlockSpec((1, HEAD_DIM), lambda b, g, i: (0, 0))
    return pl.pallas_call(
        functools.partial(_diff_attn_kernel, blk=blk, nh=nh, lambda_init=lambda_init),
        grid=(B, groups, S // blk),
        in_specs=[pl.BlockSpec((None, W, blk), lambda b, g, i: (b, g, i)),
                  pl.BlockSpec((None, S, W), lambda b, g, i: (b, 0, g)),
                  pl.BlockSpec((None, W, S), lambda b, g, i: (b, groups + g, 0)),
                  vec, vec, vec, vec,
                  pl.BlockSpec((LANES, 1), lambda b, g, i: (0, 0))],
        out_specs=pl.BlockSpec((None, blk, W), lambda b, g, i: (b, i, g)),
        out_shape=jax.ShapeDtypeStruct((B, S, BR_W), BF16),
        scratch_shapes=[pltpu.VMEM((2 * nh, 1, blk), F32),
                        pltpu.VMEM((2 * nh, LANES + ONES_ROWS, blk), F32)],
        compiler_params=_cparams("parallel", "parallel", "arbitrary"),
        name="diff_attn",
    )(yT, yn, yT, lq1, lk1, lq2, lk2, gn_col)


def _log_forget_cumsum_kernel(fg_ref, b_ref, c_ref, *, tiles):
    x = fg_ref[...] + b_ref[...]
    c = jnp.minimum(x, 0.0) - jnp.log(1.0 + jnp.exp(-jnp.abs(x)))
    rows = c.shape[0]
    lane = lax.broadcasted_iota(jnp.int32, (rows, LANES), 1)
    d = 1
    while d < LANES:
        c = c + jnp.where(lane >= d, pltpu.roll(c, d, 1), 0.0)
        d *= 2
    tile = lax.broadcasted_iota(jnp.int32, (rows, LANES), 0) % tiles
    tot = jnp.broadcast_to(c[:, LANES - 1:LANES], (rows, LANES))
    off = jnp.where(tile >= 1, pltpu.roll(tot, 1, 0), 0.0)
    d = 1
    while d < tiles:
        off = off + jnp.where(tile >= d, pltpu.roll(off, d, 0), 0.0)
        d *= 2
    c_ref[...] = (c + off) * LOG2E


def _log_forget_cumsum(fg_rows, b_rows, *, tiles):
    B, R, _ = fg_rows.shape
    return pl.pallas_call(
        functools.partial(_log_forget_cumsum_kernel, tiles=tiles),
        grid=(B,),
        in_specs=[pl.BlockSpec((None, R, LANES), lambda b: (b, 0, 0)),
                  pl.BlockSpec((R, 1), lambda b: (0, 0))],
        out_specs=pl.BlockSpec((None, R, LANES), lambda b: (b, 0, 0)),
        out_shape=jax.ShapeDtypeStruct((B, R, LANES), F32),
        compiler_params=_cparams("parallel"),
        name="log_forget_cumsum",
    )(fg_rows, b_rows)


BIAS_PIECES = 3


def _forget_attn_kernel(qT_ref, k_ref, vT_ref, c_ref, o_ref, m_ref, acc_ref, ka_ref, *, blk, npair):
    qi = pl.program_id(2)
    nhead = 2 * npair
    S = c_ref.shape[-1]

    def bias_slot(h):
        return 0 if h % 2 else HEAD_DIM

    @pl.when(qi == 0)
    def _():
        lane = lax.broadcasted_iota(jnp.int32, (S, LANES), 1)
        for h in range(nhead):
            rest = jnp.broadcast_to(c_ref[h:h + 1, :], (LANES, S)).T
            aug = jnp.zeros((S, LANES), BF16)
            for i in range(BIAS_PIECES):
                piece = rest.astype(BF16)
                rest = rest - piece.astype(F32)
                aug = jnp.where(lane == bias_slot(h) + i, piece, aug)
            own = (lane >= HEAD_DIM) if h % 2 else (lane < HEAD_DIM)
            ka_ref[h] = jnp.where(own, k_ref[:, (h // 2) * LANES:(h // 2 + 1) * LANES], aug)

    qTh = []
    row = lax.broadcasted_iota(jnp.int32, (LANES, blk), 0)
    for p in range(npair):
        for e, qh in enumerate(_split_rows(qT_ref[p * LANES:(p + 1) * LANES, :])):
            slot = bias_slot(e)
            qTh.append(jnp.where((row >= slot) & (row < slot + BIAS_PIECES), jnp.full_like(qh, -1.0), qh))
    q0 = pl.multiple_of(qi * blk, blk)
    cq = [c_ref[h:h + 1, pl.ds(q0, blk)] for h in range(nhead)]
    m_ref[...] = jnp.full(m_ref.shape, -jnp.inf, F32)
    acc_ref[...] = jnp.zeros(acc_ref.shape, F32)
    def block(start, nk, q_lo, masked):
        qs = slice(q_lo, blk)
        ones = jnp.ones((ONES_ROWS, nk), BF16)
        vTs = [jnp.concatenate([vT_ref[h * HEAD_DIM:(h + 1) * HEAD_DIM, pl.ds(start, nk)], ones], axis=0)
               for h in range(nhead)]
        zTs = [jnp.dot(ka_ref[h, pl.ds(start, nk), :], qTh[h][:, qs], preferred_element_type=F32)
               for h in range(nhead)]
        for h in range(nhead):
            zT = zTs[h]
            if masked:
                kk = lax.broadcasted_iota(jnp.int32, zT.shape, 0)
                qq = lax.broadcasted_iota(jnp.int32, zT.shape, 1)
                zT = jnp.where(kk <= qq, zT, -jnp.inf)
            _online_softmax_step(zT, m_ref, acc_ref, h, qs, vTs[h], shift=cq[h][:, qs])

    def body(j, carry):
        block(pl.multiple_of(j * blk, blk), blk, 0, False)
        return carry

    lax.fori_loop(0, qi, body, 0)
    block(q0, blk, 0, True)

    for p in range(npair):
        a0, a1 = acc_ref[2 * p], acc_ref[2 * p + 1]
        oT = jnp.concatenate([a0[:HEAD_DIM] / a0[HEAD_DIM:HEAD_DIM + 1],
                              a1[:HEAD_DIM] / a1[HEAD_DIM:HEAD_DIM + 1]], axis=0)
        o_ref[:, p * LANES:(p + 1) * LANES] = oT.T.astype(BF16)


def _forget_attn(yT, yn, c, *, blk, npair):
    B, S, _ = yn.shape
    blk = min(blk, S)
    W = npair * LANES
    groups = BR_W // W
    c4 = c.reshape(B, groups, 2 * npair, S)
    return pl.pallas_call(
        functools.partial(_forget_attn_kernel, blk=blk, npair=npair),
        grid=(B, groups, S // blk),
        in_specs=[pl.BlockSpec((None, W, blk), lambda b, g, i: (b, 2 * groups + g, i)),
                  pl.BlockSpec((None, S, W), lambda b, g, i: (b, 0, groups + g)),
                  pl.BlockSpec((None, W, S), lambda b, g, i: (b, 3 * groups + g, 0)),
                  pl.BlockSpec((None, None, 2 * npair, S), lambda b, g, i: (b, g, 0, 0))],
        out_specs=pl.BlockSpec((None, blk, W), lambda b, g, i: (b, i, g)),
        out_shape=jax.ShapeDtypeStruct((B, S, BR_W), BF16),
        scratch_shapes=[pltpu.VMEM((2 * npair, 1, blk), F32),
                        pltpu.VMEM((2 * npair, HEAD_DIM + ONES_ROWS, blk), F32),
                        pltpu.VMEM((2 * npair, S, LANES), BF16)],
        compiler_params=_cparams("parallel", "arbitrary", "arbitrary"),
        name="forget_attn",
    )(yT, yn, yT, c4)


def _conv_branches_kernel(cu_ref, cuh_ref, sx_ref, sxh_ref, sb_ref, sc_ref, sch_ref,
                          bglu_ref, cw_ref, cb_ref, lng_ref, lnb_ref, scw_ref,
                          yc_ref, ys_ref, g_ref, gs_ref, p_ref, cv_ref, *, tc, rows):
    first = pl.program_id(1) == 0
    n = tc + HALO

    def glu(u):
        u = u.astype(F32) + bglu_ref[...]
        return u[:, :BR_W] * jax.nn.sigmoid(u[:, BR_W:])

    g_ref[pl.ds(0, HALO), :] = jnp.where(first, 0.0, glu(cuh_ref[...]))
    g_ref[pl.ds(HALO, tc), :] = glu(cu_ref[...])
    g_ref[pl.ds(n, SUBLANES), :] = jnp.zeros((SUBLANES, BR_W), F32)
    for b in range(SUBLANES):
        gs_ref[b] = g_ref[pl.ds(b, n), :]

    def body(r, carry):
        r0 = pl.multiple_of(r * rows, rows)
        acc = jnp.zeros((rows, BR_W), F32)
        for k in range(CONV_K):
            a, b = divmod(HALO - (CONV_K - 1) + k, SUBLANES)
            start = pl.multiple_of(r0 + a * SUBLANES, SUBLANES)
            acc = acc + cw_ref[k:k + 1, :] * gs_ref[b, pl.ds(start, rows), :]
        cv_ref[pl.ds(r0, rows), :] = acc
        return carry

    lax.fori_loop(0, tc // rows, body, 0)
    cv = cv_ref[...] + cb_ref[...]
    mu = jnp.mean(cv, axis=-1, keepdims=True)
    var = jnp.mean(jnp.square(cv - mu), axis=-1, keepdims=True)
    yn = (cv - mu) * lax.rsqrt(var + EPS) * lng_ref[...] + lnb_ref[...]
    yc_ref[...] = (yn * jax.nn.sigmoid(yn)).astype(BF16)

    ph = sch_ref[...].astype(F32) * sxh_ref[...].astype(F32)
    p_ref[pl.ds(0, HALO), :] = jnp.where(first, 0.0, ph)
    p_ref[pl.ds(HALO, tc), :] = sc_ref[...].astype(F32) * sx_ref[...].astype(F32)
    sacc = scw_ref[0:1, :] * p_ref[pl.ds(HALO - (SC_K - 1), tc), :]
    for k in range(1, SC_K):
        sacc = sacc + scw_ref[k:k + 1, :] * p_ref[pl.ds(HALO - (SC_K - 1) + k, tc), :]
    ys_ref[...] = (sb_ref[...].astype(F32) * sacc).astype(BF16)


def _conv_branches(y3, bglu, cw, cb, lng, lnb, scw, *, tc, rows=32):
    B, S, _ = y3.shape
    tc = min(tc, S)
    cu_blk = 1
    sx_blk = 4
    hb = tc // HALO

    def cur(cb_):
        return lambda b, i: (b, i, cb_)

    def halo(cb_):
        return lambda b, i: (b, jnp.maximum(i * hb - 1, 0), cb_)

    def full(a):
        return pl.BlockSpec(a.shape, lambda b, i: (0,) * a.ndim)

    out_spec = pl.BlockSpec((None, tc, BR_W), lambda b, i: (b, i, 0))
    return pl.pallas_call(
        functools.partial(_conv_branches_kernel, tc=tc, rows=rows),
        grid=(B, S // tc),
        in_specs=[pl.BlockSpec((None, tc, 2 * BR_W), cur(cu_blk)),
                  pl.BlockSpec((None, HALO, 2 * BR_W), halo(cu_blk)),
                  pl.BlockSpec((None, tc, BR_W), cur(sx_blk)),
                  pl.BlockSpec((None, HALO, BR_W), halo(sx_blk)),
                  pl.BlockSpec((None, tc, BR_W), cur(sx_blk + 1)),
                  pl.BlockSpec((None, tc, BR_W), cur(sx_blk + 2)),
                  pl.BlockSpec((None, HALO, BR_W), halo(sx_blk + 2)),
                  full(bglu), full(cw), full(cb), full(lng), full(lnb), full(scw)],
        out_specs=[out_spec, out_spec],
        out_shape=[jax.ShapeDtypeStruct((B, S, BR_W), BF16)] * 2,
        scratch_shapes=[pltpu.VMEM((tc + HALO + SUBLANES, BR_W), F32),
                        pltpu.VMEM((SUBLANES, tc + HALO, BR_W), F32),
                        pltpu.VMEM((tc + HALO, BR_W), F32),
                        pltpu.VMEM((tc, BR_W), F32)],
        compiler_params=_cparams("parallel", "arbitrary"),
        name="conv_branches",
    )(y3, y3, y3, y3, y3, y3, y3, bglu, cw, cb, lng, lnb, scw)


def _merge_kernel(x_ref, ya_ref, yf_ref, yc_ref, ys_ref, gpre_ref, wb_ref, wg_ref, bg_ref, wo_ref,
                  gpost_ref, o_ref):
    x = x_ref[...]
    D = x.shape[-1]
    h = _rms(x, gpre_ref[...]).astype(BF16)
    mixed = None
    for n, br_ref in enumerate((ya_ref, yf_ref, yc_ref, ys_ref)):
        proj = jnp.dot(br_ref[...], wb_ref[n], preferred_element_type=F32)
        gate = jax.nn.sigmoid(jnp.dot(h, wg_ref[:, n * D:(n + 1) * D], preferred_element_type=F32)
                              + bg_ref[:, n * D:(n + 1) * D])
        mixed = gate * proj if mixed is None else mixed + gate * proj
    out = jnp.dot(mixed.astype(BF16), wo_ref[...], preferred_element_type=F32)
    o_ref[...] = x + _rms(out, gpost_ref[...])


def _merge(x2d, ya, yf, yc, ys, gpre, wb, wg, bg, wo, gpost, *, tm):
    T, D = x2d.shape
    tm = min(tm, T)
    row = lambda w: pl.BlockSpec((tm, w), lambda i: (i, 0))
    return pl.pallas_call(
        _merge_kernel,
        grid=(T // tm,),
        in_specs=[row(D), row(BR_W), row(BR_W), row(BR_W), row(BR_W),
                  _const_spec(gpre), _const_spec(wb), _const_spec(wg), _const_spec(bg),
                  _const_spec(wo), _const_spec(gpost)],
        out_specs=row(D),
        out_shape=jax.ShapeDtypeStruct((T, D), F32),
        compiler_params=_cparams("parallel"),
        name="merge",
    )(x2d, ya, yf, yc, ys, gpre, wb, wg, bg, wo, gpost)


def _mem_kv_kernel(m_ref, g_ref, w_ref, o_ref):
    m = _rms(m_ref[...], g_ref[...]).astype(BF16)
    o_ref[...] = jnp.dot(m, w_ref[...], preferred_element_type=F32).astype(BF16)


def _mem_kv(mem2d, g, w, *, tm):
    T, D = mem2d.shape
    tm = min(tm, T)
    return pl.pallas_call(
        _mem_kv_kernel,
        grid=(T // tm,),
        in_specs=[pl.BlockSpec((tm, D), lambda i: (i, 0)), _const_spec(g), _const_spec(w)],
        out_specs=pl.BlockSpec((tm, w.shape[1]), lambda i: (i, 0)),
        out_shape=jax.ShapeDtypeStruct((T, w.shape[1]), BF16),
        compiler_params=_cparams("parallel"),
        name="mem_kv",
    )(mem2d, g, w)


def _cross_attn_kernel(x_ref, kv_ref, gpre_ref, wq_ref, wo_ref, gpost_ref, o_ref, oh_ref):
    x = x_ref[...]
    hx = _rms(x, gpre_ref[...]).astype(BF16)
    q = (jnp.dot(hx, wq_ref[...], preferred_element_type=F32) * (X_HEAD_DIM ** -0.5)).astype(BF16)
    xw = X_HEADS * X_HEAD_DIM
    for h in range(X_HEADS):
        sl = slice(h * X_HEAD_DIM, (h + 1) * X_HEAD_DIM)
        s = _nt_dot(q[:, sl], kv_ref[:, sl])
        p = jnp.exp(s - jnp.max(s, axis=-1, keepdims=True))
        o = jnp.dot(p.astype(BF16), kv_ref[:, xw + h * X_HEAD_DIM:xw + (h + 1) * X_HEAD_DIM],
                    preferred_element_type=F32)
        oh_ref[:, sl] = (o / jnp.sum(p, axis=-1, keepdims=True)).astype(BF16)
    ox = jnp.dot(oh_ref[...], wo_ref[...], preferred_element_type=F32)
    o_ref[...] = x + _rms(ox, gpost_ref[...])


def _cross_attn(x2d, kv3, gpre, wq, wo, gpost, *, tm, seq):
    T, D = x2d.shape
    tm = min(tm, seq)
    per_b = seq // tm
    M, W = kv3.shape[1:]
    return pl.pallas_call(
        _cross_attn_kernel,
        grid=(T // tm,),
        in_specs=[pl.BlockSpec((tm, D), lambda i: (i, 0)),
                  pl.BlockSpec((None, M, W), lambda i: (i // per_b, 0, 0)),
                  _const_spec(gpre), _const_spec(wq), _const_spec(wo), _const_spec(gpost)],
        out_specs=pl.BlockSpec((tm, D), lambda i: (i, 0)),
        out_shape=jax.ShapeDtypeStruct((T, D), F32),
        scratch_shapes=[pltpu.VMEM((tm, X_HEADS * X_HEAD_DIM), BF16)],
        compiler_params=_cparams("parallel"),
        name="cross_attn",
    )(x2d, kv3, gpre, wq, wo, gpost)


FFN_CHUNK = 256


def _ffn_kernel(x_ref, xh_ref, gpre_ref, wu_ref, dw_ref, db_ref, wd_ref, gpost_ref, o_ref, hf_ref,
                *, tm, per_b):
    first = (pl.program_id(0) % per_b) == 0
    d_ff = wd_ref.shape[0]
    x = x_ref[...]
    hf_ref[pl.ds(0, FFN_HALO), :] = jnp.where(first, 0.0, _rms(xh_ref[...], gpre_ref[...])).astype(BF16)
    hf_ref[pl.ds(FFN_HALO, tm), :] = _rms(x, gpre_ref[...]).astype(BF16)

    def cols(c, half):
        return slice(half * d_ff + c * FFN_CHUNK, half * d_ff + (c + 1) * FFN_CHUNK)

    def up_proj(c):
        hf = hf_ref[...]
        return tuple(jnp.dot(hf, wu_ref[:, cols(c, half)], preferred_element_type=F32) for half in range(2))

    def conv(up, sl):
        u = dw_ref[FFN_K - 1:FFN_K, sl] * up[FFN_HALO:, :]
        for k in range(FFN_K - 1):
            sh = FFN_K - 1 - k
            u = u + dw_ref[k:k + 1, sl] * up[FFN_HALO - sh:FFN_HALO - sh + tm, :]
        return u + db_ref[:, sl]

    nch = d_ff // FFN_CHUNK
    acts = []
    ups = up_proj(0)
    for c in range(nch):
        nxt = up_proj(c + 1) if c + 1 < nch else None
        ug = conv(ups[0], cols(c, 0))
        uv = conv(ups[1], cols(c, 1))
        acts.append((ug * jax.nn.sigmoid(ug) * uv).astype(BF16))
        ups = nxt
    out = jnp.dot(jnp.concatenate(acts, axis=1), wd_ref[...], preferred_element_type=F32)
    o_ref[...] = x + _rms(out, gpost_ref[...])


def _ffn(x2d, gpre, wu, dw, db, wd, gpost, *, tm, seq):
    T, D = x2d.shape
    tm = min(tm, seq)
    per_b = seq // tm
    hb = tm // FFN_HALO
    return pl.pallas_call(
        functools.partial(_ffn_kernel, tm=tm, per_b=per_b),
        grid=(T // tm,),
        in_specs=[pl.BlockSpec((tm, D), lambda i: (i, 0)),
                  pl.BlockSpec((FFN_HALO, D), lambda i: (jnp.maximum(i * hb - 1, 0), 0)),
                  _const_spec(gpre), _const_spec(wu), _const_spec(dw), _const_spec(db), _const_spec(wd),
                  _const_spec(gpost)],
        out_specs=pl.BlockSpec((tm, D), lambda i: (i, 0)),
        out_shape=jax.ShapeDtypeStruct((T, D), F32),
        scratch_shapes=[pltpu.VMEM((tm + FFN_HALO, D), BF16)],
        compiler_params=_cparams("parallel"),
        name="ffn",
    )(x2d, x2d, gpre, wu, dw, db, wd, gpost)


def kernel(x, mem, norm_mix_pre, norm_mix_post, w_in, b_fgt, lam_q1, lam_k1, lam_q2, lam_k2, diff_norm, b_glu, conv_dw, conv_dw_b, conv_ln_g, conv_ln_b, sc_w, w_branch, w_gate, b_gate, w_out, norm_x_pre, norm_x_post, norm_mem, w_xq, w_xkv, w_xo, norm_ffn_pre, norm_ffn_post, w_up, ffn_dw, ffn_dw_b, w_down):
    B, S, D = x.shape
    depth = w_in.shape[0]
    T = B * S
    fg_lo = 6 * BR_W
    fg_hi = fg_lo + F_HEADS
    tiles = S // LANES
    row = lambda a: a.reshape(1, -1)

    q_scale = (HEAD_DIM ** -0.5) * LOG2E
    x2d = x.reshape(T, D)
    mem2d = mem.reshape(-1, D)
    for l in range(depth):
        lambda_init = 0.8 - 0.6 * math.exp(-0.3 * l)
        wl = w_in[l]
        seg = lambda i: wl[:, i * BR_W:(i + 1) * BR_W]
        w_nat = jnp.concatenate([seg(1), seg(4), wl[:, fg_hi:]], axis=1).astype(BF16)
        w_tr = jnp.concatenate([seg(0), seg(2), seg(3), seg(5)], axis=1).T.astype(BF16)
        w_fg = jnp.pad(wl[:, fg_lo:fg_hi], ((0, 0), (0, LANES - F_HEADS))).astype(BF16)

        yn, yT, fg = _in_proj(x2d, row(norm_mix_pre[l]), w_nat, w_tr, w_fg, tm=512, seq=S, q_scale=q_scale)
        y3 = yn.reshape(B, S, -1)
        ya = _diff_attn(yT, y3, row(lam_q1[l]), row(lam_k1[l]), row(lam_q2[l]), row(lam_k2[l]),
                        diff_norm[l].reshape(-1, 1), blk=512, nh=2, lambda_init=lambda_init)
        fg_rows = fg.reshape(B, S, LANES)[:, :, :F_HEADS].transpose(0, 2, 1).reshape(B, F_HEADS * tiles, LANES)
        b_rows = jnp.repeat(b_fgt[l], tiles).reshape(F_HEADS * tiles, 1)
        c = _log_forget_cumsum(fg_rows, b_rows, tiles=tiles).reshape(B, F_HEADS, S)
        yf = _forget_attn(yT, y3, c, blk=512, npair=2)
        yc, ys = _conv_branches(y3, row(b_glu[l]), conv_dw[l], row(conv_dw_b[l]), row(conv_ln_g[l]),
                                row(conv_ln_b[l]), sc_w[l], tc=512)
        x2d = _merge(x2d, ya.reshape(T, BR_W), yf.reshape(T, BR_W), yc.reshape(T, BR_W),
                     ys.reshape(T, BR_W), row(norm_mix_pre[l]), w_branch[l].astype(BF16),
                     w_gate[l].astype(BF16), row(b_gate[l]), w_out[l].astype(BF16),
                     row(norm_mix_post[l]), tm=512)

        kv = _mem_kv(mem2d, row(norm_mem[l]), w_xkv[l].astype(BF16), tm=256)
        x2d = _cross_attn(x2d, kv.reshape(B, -1, kv.shape[-1]), row(norm_x_pre[l]), w_xq[l].astype(BF16),
                          w_xo[l].astype(BF16), row(norm_x_post[l]), tm=512, seq=S)

        x2d = _ffn(x2d, row(norm_ffn_pre[l]), w_up[l].astype(BF16), ffn_dw[l], row(ffn_dw_b[l]),
                   w_down[l].astype(BF16), row(norm_ffn_post[l]), tm=1024, seq=S)
    return x2d.reshape(B, S, D)
```
